```python
import math
import jax, jax.numpy as jnp
from jax import lax
import numpy as np

D_MODEL = 1024
BATCH = 2
SEQ = 16384
DEPTH = 1

N_MEM = 256
HEAD_DIM = 64
DIFF_HEADS = D_MODEL // (2 * HEAD_DIM)
DIFF_V_DIM = 2 * HEAD_DIM
MOBA_HEADS = D_MODEL // (2 * HEAD_DIM)
MOBA_BLOCK = 256
MOBA_TOPK = 3
Q_BLOCK = 128
XATTN_HEADS = 4
XATTN_HEAD_DIM = D_MODEL // XATTN_HEADS
D_FF = 4 * D_MODEL
REL_BUCKETS = 32
REL_MAX_DIST = 128
N_SELF_HEADS = DIFF_HEADS + MOBA_HEADS
LN_EPS = 1e-5
DEEPNORM_ALPHA = (2.0 * DEPTH) ** 0.25
DEEPNORM_BETA = (8.0 * DEPTH) ** -0.25

COL_DQ = DIFF_HEADS * 2 * HEAD_DIM
COL_DK = DIFF_HEADS * 2 * HEAD_DIM
COL_DV = DIFF_HEADS * DIFF_V_DIM
COL_MQ = MOBA_HEADS * HEAD_DIM
COL_MK = MOBA_HEADS * HEAD_DIM
COL_MV = MOBA_HEADS * HEAD_DIM
COL_GATE = 2 * D_MODEL
W_IN_COLS = COL_DQ + COL_DK + COL_DV + COL_MQ + COL_MK + COL_MV + COL_GATE

kernel_name = "hybrid_diffattn_moba_gated_deepnorm"


def _split_points():
    widths = [COL_DQ, COL_DK, COL_DV, COL_MQ, COL_MK, COL_MV]
    return [int(v) for v in np.cumsum(widths)]


def layer_norm(x, g, b):
    xf = x.astype(jnp.float32)
    mu = jnp.mean(xf, axis=-1, keepdims=True)
    var = jnp.mean(jnp.square(xf - mu), axis=-1, keepdims=True)
    return ((xf - mu) * lax.rsqrt(var + LN_EPS) * g.astype(jnp.float32) + b.astype(jnp.float32)).astype(x.dtype)


def rel_bucket(n):
    max_exact = REL_BUCKETS // 2
    nf = jnp.maximum(n, 1).astype(jnp.float32)
    large = max_exact + (jnp.log(nf / max_exact) / math.log(REL_MAX_DIST / max_exact)
                         * (REL_BUCKETS - max_exact)).astype(jnp.int32)
    large = jnp.minimum(large, REL_BUCKETS - 1)
    return jnp.where(n < max_exact, n, large)


def diff_attention(q, k, v, bias_by_dist, lam, sub_g, lam_init):
    B, H, _, S, Dh = q.shape
    n_qb = S // Q_BLOCK
    kpos = jnp.arange(S)
    scale = HEAD_DIM ** -0.5

    def block(i):
        start = i * Q_BLOCK
        qb = lax.dynamic_slice_in_dim(q, start, Q_BLOCK, axis=3)
        qpos = start + jnp.arange(Q_BLOCK)
        dist = qpos[:, None] - kpos[None, :]
        bias = bias_by_dist[:, jnp.clip(dist, 0, S - 1)]
        logits = jnp.einsum('bhmqd,bhmkd->bhmqk', qb, k,
                            preferred_element_type=jnp.float32) * scale + bias[None, :, None]
        logits = jnp.where(dist >= 0, logits, -jnp.inf)
        p = jax.nn.softmax(logits, axis=-1)
        a = p[:, :, 0] - lam * p[:, :, 1]
        return jnp.einsum('bhqk,bhkd->bhqd', a.astype(v.dtype), v)

    out = lax.map(block, jnp.arange(n_qb))
    out = out.transpose(1, 2, 0, 3, 4).reshape(B, H, S, 2 * Dh)
    of = out.astype(jnp.float32)
    of = of * lax.rsqrt(jnp.mean(of * of, axis=-1, keepdims=True) + LN_EPS) * sub_g.astype(jnp.float32)
    of = of * (1.0 - lam_init)
    return of.transpose(0, 2, 1, 3).reshape(B, S, H * 2 * Dh).astype(v.dtype)


def moba_attention(q, k, v, bias_by_dist):
    B, H, S, Dh = q.shape
    nb = max(-(-S // MOBA_BLOCK), MOBA_TOPK)
    pad = nb * MOBA_BLOCK - S
    kp = jnp.pad(k, ((0, 0), (0, 0), (0, pad), (0, 0))).reshape(B, H, nb, MOBA_BLOCK, Dh)
    vp = jnp.pad(v, ((0, 0), (0, 0), (0, pad), (0, 0))).reshape(B, H, nb, MOBA_BLOCK, Dh)
    k_mean = jnp.mean(kp.astype(jnp.float32), axis=3)
    scale = Dh ** -0.5
    blk_ids = jnp.arange(nb)
    offs = jnp.arange(MOBA_BLOCK)
    b_idx = jnp.arange(B)[:, None, None, None]
    h_idx = jnp.arange(H)[None, :, None, None]
    h_idx5 = jnp.arange(H)[None, :, None, None, None]
    rank_valid_ids = jnp.arange(MOBA_TOPK)

    def chunk(i):
        start = i * Q_BLOCK
        cur = start // MOBA_BLOCK
        qc = lax.dynamic_slice_in_dim(q, start, Q_BLOCK, axis=2)
        qpos = start + jnp.arange(Q_BLOCK)
        gate = jnp.einsum('bhqd,bhnd->bhqn', qc.astype(jnp.float32), k_mean)
        gate = jnp.where(blk_ids < cur, gate, -jnp.inf)
        _, sel = lax.top_k(gate, MOBA_TOPK)
        sel_valid = rank_valid_ids < cur
        ks = kp[b_idx, h_idx, sel]
        vs = vp[b_idx, h_idx, sel]
        l_sel = jnp.einsum('bhqd,bhqnkd->bhqnk', qc, ks,
                           preferred_element_type=jnp.float32) * scale
        kpos_sel = sel[..., None] * MOBA_BLOCK + offs
        dist_sel = jnp.clip(qpos[None, None, :, None, None] - kpos_sel, 0, S - 1)
        l_sel = l_sel + bias_by_dist[h_idx5, dist_sel]
        l_sel = jnp.where(sel_valid[None, None, None, :, None], l_sel, -jnp.inf)
        ko = lax.dynamic_index_in_dim(kp, cur, axis=2, keepdims=False)
        vo = lax.dynamic_index_in_dim(vp, cur, axis=2, keepdims=False)
        kpos_own = cur * MOBA_BLOCK + offs
        dist_own = qpos[:, None] - kpos_own[None, :]
        l_own = jnp.einsum('bhqd,bhkd->bhqk', qc, ko,
                           preferred_element_type=jnp.float32) * scale
        l_own = l_own + bias_by_dist[:, jnp.clip(dist_own, 0, S - 1)][None]
        l_own = jnp.where(dist_own >= 0, l_own, -jnp.inf)
        logits = jnp.concatenate([l_sel.reshape(B, H, Q_BLOCK, MOBA_TOPK * MOBA_BLOCK), l_own], axis=-1)
        p = jax.nn.softmax(logits, axis=-1).astype(v.dtype)
        p_sel = p[..., :MOBA_TOPK * MOBA_BLOCK].reshape(B, H, Q_BLOCK, MOBA_TOPK, MOBA_BLOCK)
        p_own = p[..., MOBA_TOPK * MOBA_BLOCK:]
        return (jnp.einsum('bhqnk,bhqnkd->bhqd', p_sel, vs)
                + jnp.einsum('bhqk,bhkd->bhqd', p_own, vo))

    out = lax.map(chunk, jnp.arange(S // Q_BLOCK))
    return out.transpose(1, 0, 3, 2, 4).reshape(B, S, H * Dh)


def memory_cross_attention(h, mem, wq, wk, wv, wo):
    B, S, D = h.shape
    M = mem.shape[1]
    q = (h @ wq).reshape(B, S, XATTN_HEADS, XATTN_HEAD_DIM)
    k = (mem @ wk).reshape(B, M, XATTN_HEADS, XATTN_HEAD_DIM)
    v = (mem @ wv).reshape(B, M, XATTN_HEADS, XATTN_HEAD_DIM)
    logits = jnp.einsum('bshd,bmhd->bhsm', q, k, preferred_element_type=jnp.float32) * (XATTN_HEAD_DIM ** -0.5)
    p = jax.nn.softmax(logits, axis=-1).astype(v.dtype)
    o = jnp.einsum('bhsm,bmhd->bshd', p, v).reshape(B, S, D)
    return o @ wo


def setup_inputs(seed: int = 0) -> dict:
    key = jax.random.key(seed)
    ks = list(jax.random.split(key, 40))
    nrm = jax.random.normal
    D = D_MODEL
    sd = D ** -0.5
    beta = DEEPNORM_BETA
    widths = [COL_DQ, COL_DK, COL_DV, COL_MQ, COL_MK, COL_MV, COL_GATE]
    scales = [sd, sd, sd * beta, sd, sd, sd * beta, sd]
    w_in = jnp.concatenate([nrm(ks[20 + i], (DEPTH, D, w), jnp.float32) * s
                            for i, (w, s) in enumerate(zip(widths, scales))], axis=-1)

    def gain(k, n):
        return 1.0 + 0.02 * nrm(k, n, jnp.float32)

    def bias(k, n):
        return 0.02 * nrm(k, n, jnp.float32)

    return {
        "x": nrm(ks[0], (BATCH, SEQ, D), jnp.float32),
        "mem": nrm(ks[1], (BATCH, N_MEM, D), jnp.float32),
        "ln_in_g": gain(ks[2], (D,)),
        "ln_in_b": bias(ks[3], (D,)),
        "rel_table": 0.3 * nrm(ks[4], (REL_BUCKETS, N_SELF_HEADS), jnp.float32),
        "w_in": w_in,
        "b_gate": 0.01 * nrm(ks[5], (DEPTH, COL_GATE), jnp.float32),
        "lam_q1": 0.1 * nrm(ks[6], (DEPTH, HEAD_DIM), jnp.float32),
        "lam_k1": 0.1 * nrm(ks[7], (DEPTH, HEAD_DIM), jnp.float32),
        "lam_q2": 0.1 * nrm(ks[8], (DEPTH, HEAD_DIM), jnp.float32),
        "lam_k2": 0.1 * nrm(ks[9], (DEPTH, HEAD_DIM), jnp.float32),
        "diff_sub_g": gain(ks[10], (DEPTH, DIFF_V_DIM)),
        "w_br_diff": nrm(ks[11], (DEPTH, COL_DV, D), jnp.float32) * COL_DV ** -0.5,
        "w_br_moba": nrm(ks[12], (DEPTH, COL_MV, D), jnp.float32) * COL_MV ** -0.5,
        "w_out": nrm(ks[13], (DEPTH, D, D), jnp.float32) * sd * beta,
        "ln1_g": gain(ks[14], (DEPTH, D)),
        "ln1_b": bias(ks[15], (DEPTH, D)),
        "wq_x": nrm(ks[16], (DEPTH, D, D), jnp.float32) * sd,
        "wk_x": nrm(ks[17], (DEPTH, D, D), jnp.float32) * sd,
        "wv_x": nrm(ks[18], (DEPTH, D, D), jnp.float32) * sd * beta,
        "wo_x": nrm(ks[19], (DEPTH, D, D), jnp.float32) * sd * beta,
        "ln2_g": gain(ks[30], (DEPTH, D)),
        "ln2_b": bias(ks[31], (DEPTH, D)),
        "w_ff1": nrm(ks[32], (DEPTH, D, D_FF), jnp.float32) * sd * beta,
        "w_ff2": nrm(ks[33], (DEPTH, D_FF, D), jnp.float32) * D_FF ** -0.5 * beta,
        "ln3_g": gain(ks[34], (DEPTH, D)),
        "ln3_b": bias(ks[35], (DEPTH, D)),
    }


def reference(x, mem, ln_in_g, ln_in_b, rel_table, w_in, b_gate, lam_q1, lam_k1, lam_q2, lam_k2,
              diff_sub_g, w_br_diff, w_br_moba, w_out, ln1_g, ln1_b, wq_x, wk_x, wv_x, wo_x,
              ln2_g, ln2_b, w_ff1, w_ff2, ln3_g, ln3_b):
    B, S, D = x.shape
    bias_by_dist = rel_table[rel_bucket(jnp.arange(S))].T.astype(jnp.float32)
    bias_diff = bias_by_dist[:DIFF_HEADS]
    bias_moba = bias_by_dist[DIFF_HEADS:]
    splits = _split_points()

    h = layer_norm(x, ln_in_g, ln_in_b)
    for l in range(DEPTH):
        lam_init = 0.8 - 0.6 * math.exp(-0.3 * l)
        proj = h @ w_in[l]
        dq, dk, dv, mq, mk, mv, g_logit = jnp.split(proj, splits, axis=-1)
        dq = dq.reshape(B, S, DIFF_HEADS, 2, HEAD_DIM).transpose(0, 2, 3, 1, 4)
        dk = dk.reshape(B, S, DIFF_HEADS, 2, HEAD_DIM).transpose(0, 2, 3, 1, 4)
        dv = dv.reshape(B, S, DIFF_HEADS, DIFF_V_DIM).transpose(0, 2, 1, 3)
        lam = (jnp.exp(jnp.sum(lam_q1[l].astype(jnp.float32) * lam_k1[l].astype(jnp.float32)))
               - jnp.exp(jnp.sum(lam_q2[l].astype(jnp.float32) * lam_k2[l].astype(jnp.float32)))
               + lam_init)
        y_diff = diff_attention(dq, dk, dv, bias_diff, lam, diff_sub_g[l], lam_init) @ w_br_diff[l]
        mq = mq.reshape(B, S, MOBA_HEADS, HEAD_DIM).transpose(0, 2, 1, 3)
        mk = mk.reshape(B, S, MOBA_HEADS, HEAD_DIM).transpose(0, 2, 1, 3)
        mv = mv.reshape(B, S, MOBA_HEADS, HEAD_DIM).transpose(0, 2, 1, 3)
        y_moba = moba_attention(mq, mk, mv, bias_moba) @ w_br_moba[l]
        gates = jax.nn.sigmoid(g_logit + b_gate[l])
        g_diff, g_moba = jnp.split(gates, 2, axis=-1)
        mixed = (g_diff * y_diff + g_moba * y_moba) @ w_out[l]
        h = layer_norm(DEEPNORM_ALPHA * h + mixed, ln1_g[l], ln1_b[l])
        xa = memory_cross_attention(h, mem, wq_x[l], wk_x[l], wv_x[l], wo_x[l])
        h = layer_norm(DEEPNORM_ALPHA * h + xa, ln2_g[l], ln2_b[l])
        ff = jnp.square(jax.nn.relu(h @ w_ff1[l])) @ w_ff2[l]
        h = layer_norm(DEEPNORM_ALPHA * h + ff, ln3_g[l], ln3_b[l])
    return h
```

```python
import functools
import math

import numpy as np
import jax
import jax.numpy as jnp
from jax import lax
from jax.experimental import pallas as pl
from jax.experimental.pallas import tpu as pltpu

D_MODEL = 1024
HEAD_DIM = 64
DIFF_HEADS = 8
MOBA_HEADS = 8
MOBA_BLOCK = 256
MOBA_TOPK = 3
XATTN_HEADS = 4
XATTN_HEAD_DIM = D_MODEL // XATTN_HEADS
D_FF = 4 * D_MODEL
REL_BUCKETS = 32
REL_MAX_DIST = 128
LN_EPS = 1e-5
DEPTH = 1
DEEPNORM_ALPHA = (2.0 * DEPTH) ** 0.25
LAM_INIT = 0.8 - 0.6 * math.exp(-0.3 * 0)

COL_DQ = 0
COL_DK = 1024
COL_DV = 2048
COL_MQ = 3072
COL_MK = 3584
COL_MV = 4096
COL_GATE = 4608
N_QKV_COLS = 4608
W_IN_COLS = 6656

LANES = 128
VMEM_LIMIT_BYTES = 56 * 1024 * 1024

ROW_TILE = 512
DIFF_TILE = 512
MOBA_TILE = MOBA_BLOCK
SEL_TILE = 512
FF_TILE = 1024
PROJ_CHUNK = 512

MASK_NEG = -1e30


def _rel_thresholds():
    max_exact = REL_BUCKETS // 2
    n = np.arange(0, 4 * REL_MAX_DIST)
    nf = np.maximum(n, 1).astype(np.float32)
    large = max_exact + (np.log(nf / max_exact) / math.log(REL_MAX_DIST / max_exact)
                         * (REL_BUCKETS - max_exact)).astype(np.int32)
    large = np.minimum(large, REL_BUCKETS - 1)
    bucket = np.where(n < max_exact, n, large)
    thr = [int(np.argmax(bucket >= k)) for k in range(REL_BUCKETS)]
    assert all(bucket[t] == k for k, t in enumerate(thr))
    return thr


REL_THRESHOLDS = _rel_thresholds()
REL_FAR_DIST = REL_THRESHOLDS[-1]


def _layer_norm(z, g, b):
    mu = jnp.mean(z, axis=-1, keepdims=True)
    zc = z - mu
    var = jnp.mean(zc * zc, axis=-1, keepdims=True)
    return zc * lax.rsqrt(var + LN_EPS) * g + b


def _compiler_params(n_grid):
    return pltpu.CompilerParams(
        dimension_semantics=("arbitrary",) * n_grid,
        vmem_limit_bytes=VMEM_LIMIT_BYTES,
    )


def _resident(shape):
    nd = len(shape)
    return pl.BlockSpec(shape, lambda *_: (0,) * nd, pipeline_mode=pl.Buffered(1))


def _ln_inproj_kernel(x_ref, g_ref, b_ref, w_ref, h_ref, qkv_ref, mqf_ref, gl_ref, kmean_ref):
    h = _layer_norm(x_ref[...], g_ref[...], b_ref[...])
    h_ref[...] = h
    hb = h.astype(jnp.bfloat16)
    q_scale = HEAD_DIM ** -0.5
    for c0 in range(0, W_IN_COLS, PROJ_CHUNK):
        acc = jnp.dot(hb, w_ref[:, c0:c0 + PROJ_CHUNK], preferred_element_type=jnp.float32)
        if c0 < COL_GATE:
            is_q = c0 < COL_DK or COL_MQ <= c0 < COL_MK
            if COL_MQ <= c0 < COL_MK:
                mqf_ref[:, c0 - COL_MQ:c0 - COL_MQ + PROJ_CHUNK] = acc
            if COL_MK <= c0 < COL_MV:
                nblk = acc.shape[0] // MOBA_BLOCK
                km = jnp.mean(acc.reshape(nblk, MOBA_BLOCK, PROJ_CHUNK), axis=1)
                kmean_ref[0, :, c0 - COL_MK:c0 - COL_MK + PROJ_CHUNK] = km
            val = acc * q_scale if is_q else acc
            qkv_ref[:, c0:c0 + PROJ_CHUNK] = val.astype(jnp.bfloat16)
        else:
            gl_ref[:, c0 - COL_GATE:c0 - COL_GATE + PROJ_CHUNK] = acc


def _ln_inproj(x2, ln_g, ln_b, w_in_bf):
    n_rows = x2.shape[0]
    tm = ROW_TILE
    nblk = tm // MOBA_BLOCK
    grid = (n_rows // tm,)
    row = lambda cols: pl.BlockSpec((tm, cols), lambda i: (i, 0))
    return pl.pallas_call(
        _ln_inproj_kernel,
        grid=grid,
        in_specs=[row(D_MODEL), _resident((1, D_MODEL)), _resident((1, D_MODEL)),
                  _resident((D_MODEL, W_IN_COLS))],
        out_specs=[row(D_MODEL), row(N_QKV_COLS), row(MOBA_HEADS * HEAD_DIM), row(2 * D_MODEL),
                   pl.BlockSpec((1, nblk, MOBA_HEADS * HEAD_DIM), lambda i: (i, 0, 0))],
        out_shape=[
            jax.ShapeDtypeStruct((n_rows, D_MODEL), jnp.float32),
            jax.ShapeDtypeStruct((n_rows, N_QKV_COLS), jnp.bfloat16),
            jax.ShapeDtypeStruct((n_rows, MOBA_HEADS * HEAD_DIM), jnp.float32),
            jax.ShapeDtypeStruct((n_rows, 2 * D_MODEL), jnp.float32),
            jax.ShapeDtypeStruct((n_rows // tm, nblk, MOBA_HEADS * HEAD_DIM), jnp.float32),
        ],
        compiler_params=_compiler_params(1),
        name="ln_inproj",
    )(x2, ln_g, ln_b, w_in_bf)


def _moba_select_kernel(q_ref, km_ref, neg_ref, *, n_blocks):
    t = q_ref.shape[0]
    half = LANES // 2
    row0 = pl.program_id(1) * t
    lane = lax.broadcasted_iota(jnp.int32, (t, LANES), 1)
    blk = jnp.where(lane < half, lane, lane - half)
    cur = (row0 + lax.broadcasted_iota(jnp.int32, (t, LANES), 0)) // MOBA_BLOCK
    first = lane < half
    ninf = jnp.float32(-jnp.inf)
    for pair in range(MOBA_HEADS // 2):
        q = q_ref[:, pair * LANES:(pair + 1) * LANES]
        gate = jnp.dot(q, km_ref[0, pair], preferred_element_type=jnp.float32,
                       precision=lax.Precision.HIGHEST)
        g = jnp.where((blk < cur) & (blk < n_blocks), gate, ninf)
        neg = jnp.full((t, LANES), MASK_NEG, jnp.float32)
        for rank in range(MOBA_TOPK):
            mx_a = jnp.max(jnp.where(first, g, ninf), axis=1, keepdims=True)
            mx_b = jnp.max(jnp.where(first, ninf, g), axis=1, keepdims=True)
            mx = jnp.where(first, mx_a, mx_b)
            cand = jnp.where(g == mx, blk, half)
            ix_a = jnp.min(jnp.where(first, cand, half), axis=1, keepdims=True)
            ix_b = jnp.min(jnp.where(first, half, cand), axis=1, keepdims=True)
            pick = blk == jnp.where(first, ix_a, ix_b)
            neg = jnp.where(pick & (rank < cur), 0.0, neg)
            g = jnp.where(pick, ninf, g)
        neg_ref[:, pair * LANES:(pair + 1) * LANES] = neg


def _moba_select(mqf, km2, batch, seq):
    t = SEL_TILE
    nq = seq // t
    n_blocks = seq // MOBA_BLOCK
    return pl.pallas_call(
        functools.partial(_moba_select_kernel, n_blocks=n_blocks),
        grid=(batch, nq),
        in_specs=[pl.BlockSpec((t, MOBA_HEADS * HEAD_DIM), lambda b, i: (b * nq + i, 0)),
                  pl.BlockSpec((1, MOBA_HEADS // 2, LANES, LANES), lambda b, i: (b, 0, 0, 0))],
        out_specs=pl.BlockSpec((t, MOBA_HEADS * HEAD_DIM), lambda b, i: (b * nq + i, 0)),
        out_shape=jax.ShapeDtypeStruct(mqf.shape, jnp.float32),
        compiler_params=_compiler_params(2),
        name="moba_select",
    )(mqf, km2)


def _fill_bias_table(tab_ref, rel_ref, head, t):
    far = rel_ref[head, REL_BUCKETS - 1]
    r = lax.broadcasted_iota(jnp.int32, (t, 2 * t), 0)
    c = lax.broadcasted_iota(jnp.int32, (t, 2 * t), 1)
    dist = r - c + t
    bias = jnp.full((t, 2 * t), rel_ref[head, 0] - far, jnp.float32)
    for k in range(1, REL_BUCKETS):
        bias = jnp.where(dist >= REL_THRESHOLDS[k], rel_ref[head, k] - far, bias)
    tab_ref[...] = jnp.where(dist >= 0, bias, MASK_NEG)


def _softmax_step(s, v, m_ref, l_ref, acc_ref):
    m_prev = m_ref[...]
    m_new = jnp.maximum(m_prev, jnp.max(s, axis=1, keepdims=True))
    alpha = jnp.exp(m_prev - m_new)
    p = jnp.exp(s - m_new)
    l_ref[...] = alpha * l_ref[...] + jnp.sum(p, axis=1, keepdims=True)
    acc_ref[...] = alpha * acc_ref[...] + jnp.dot(
        p.astype(jnp.bfloat16), v, preferred_element_type=jnp.float32)
    m_ref[...] = m_new


def _qk(q2, k):
    return lax.dot_general(q2, k, (((1,), (1,)), ((), ())), preferred_element_type=jnp.float32)


def _split_halves(q):
    lane = lax.broadcasted_iota(jnp.int32, q.shape, 1)
    zero = jnp.zeros_like(q)
    return jnp.concatenate([jnp.where(lane < LANES // 2, q, zero),
                            jnp.where(lane < LANES // 2, zero, q)], axis=0)


def _diff_attn_kernel(rel_ref, q_ref, k_ref, v_ref, lam_ref, subg_ref, o_ref,
                      tab_ref, m_ref, l_ref, acc_ref):
    t = q_ref.shape[0]
    head = pl.program_id(1)
    qi = pl.program_id(2)

    @pl.when(qi == 0)
    def _():
        _fill_bias_table(tab_ref, rel_ref, head, t)

    q2 = _split_halves(q_ref[...])

    def key_tile(j):
        start = pl.multiple_of(j * t, t)
        return k_ref[pl.ds(start, t), :], v_ref[pl.ds(start, t), :]

    def near_step(j, col0):
        k, v = key_tile(j)
        bias = tab_ref[:, col0:col0 + t]
        s = (_qk(q2, k).reshape(2, t, t) + bias[None]).reshape(2 * t, t)
        _softmax_step(s, v, m_ref, l_ref, acc_ref)

    m_ref[...] = jnp.full(m_ref.shape, MASK_NEG, jnp.float32)
    l_ref[...] = jnp.zeros(l_ref.shape, jnp.float32)
    acc_ref[...] = jnp.zeros(acc_ref.shape, jnp.float32)
    near_step(qi, t)

    @pl.when(qi >= 1)
    def _():
        near_step(qi - 1, 0)

    def far_step(j, carry):
        k, v = key_tile(j)
        _softmax_step(_qk(q2, k), v, m_ref, l_ref, acc_ref)
        return carry

    lax.fori_loop(0, jnp.maximum(qi - 1, 0), far_step, 0)

    lam_p = lam_ref[...]
    lam = (jnp.exp(jnp.sum(lam_p[0:1] * lam_p[1:2], axis=1, keepdims=True))
           - jnp.exp(jnp.sum(lam_p[2:3] * lam_p[3:4], axis=1, keepdims=True)) + LAM_INIT)
    o = acc_ref[...] / l_ref[...]
    o = o[:t] - lam * o[t:]
    o = o * lax.rsqrt(jnp.mean(o * o, axis=1, keepdims=True) + LN_EPS) * subg_ref[...]
    o_ref[...] = (o * (1.0 - LAM_INIT)).astype(o_ref.dtype)


def _diff_attn(rel_t, qkv, lam_p, sub_g, batch, seq):
    t = DIFF_TILE
    nq = seq // t
    grid_spec = pltpu.PrefetchScalarGridSpec(
        num_scalar_prefetch=1,
        grid=(batch, DIFF_HEADS, nq),
        in_specs=[
            pl.BlockSpec((t, LANES), lambda b, h, i, rel: (b * nq + i, COL_DQ // LANES + h)),
            pl.BlockSpec((seq, LANES), lambda b, h, i, rel: (b, COL_DK // LANES + h)),
            pl.BlockSpec((seq, LANES), lambda b, h, i, rel: (b, COL_DV // LANES + h)),
            pl.BlockSpec((4, HEAD_DIM), lambda b, h, i, rel: (0, 0)),
            pl.BlockSpec((1, LANES), lambda b, h, i, rel: (0, 0)),
        ],
        out_specs=pl.BlockSpec((t, LANES), lambda b, h, i, rel: (b * nq + i, h)),
        scratch_shapes=[
            pltpu.VMEM((t, 2 * t), jnp.float32),
            pltpu.VMEM((2 * t, 1), jnp.float32),
            pltpu.VMEM((2 * t, 1), jnp.float32),
            pltpu.VMEM((2 * t, LANES), jnp.float32),
        ],
    )
    return pl.pallas_call(
        _diff_attn_kernel,
        grid_spec=grid_spec,
        out_shape=jax.ShapeDtypeStruct((batch * seq, DIFF_HEADS * LANES), jnp.bfloat16),
        compiler_params=_compiler_params(3),
        name="diff_attn",
    )(rel_t, qkv, qkv, qkv, lam_p, sub_g)


def _moba_attn_kernel(rel_ref, q_ref, k_ref, v_ref, neg_ref, o_ref,
                      tab_ref, neg2_ref, m_ref, l_ref, acc_ref):
    t = q_ref.shape[0]
    half = LANES // 2
    pair = pl.program_id(1)
    qi = pl.program_id(2)

    @pl.when(qi == 0)
    def _():
        _fill_bias_table(tab_ref.at[0], rel_ref, DIFF_HEADS + 2 * pair, t)
        _fill_bias_table(tab_ref.at[1], rel_ref, DIFF_HEADS + 2 * pair + 1, t)

    q2 = _split_halves(q_ref[...])
    neg = neg_ref[...]
    neg2_ref[0:t, :] = neg
    neg2_ref[t:2 * t, :] = pltpu.roll(neg, half, axis=1)
    lane = lax.broadcasted_iota(jnp.int32, (2 * t, LANES), 1)

    def key_tile(j):
        start = pl.multiple_of(j * t, t)
        return k_ref[pl.ds(start, t), :], v_ref[pl.ds(start, t), :]

    def block_mask(j):
        return jnp.sum(jnp.where(lane == j, neg2_ref[...], 0.0), axis=1, keepdims=True)

    m_ref[...] = jnp.full(m_ref.shape, MASK_NEG, jnp.float32)
    l_ref[...] = jnp.zeros(l_ref.shape, jnp.float32)
    acc_ref[...] = jnp.zeros(acc_ref.shape, jnp.float32)

    k, v = key_tile(qi)
    s = (_qk(q2, k).reshape(2, t, t) + tab_ref[:, :, t:2 * t]).reshape(2 * t, t)
    _softmax_step(s, v, m_ref, l_ref, acc_ref)

    @pl.when(qi >= 1)
    def _():
        k, v = key_tile(qi - 1)
        s = (_qk(q2, k).reshape(2, t, t) + tab_ref[:, :, 0:t]).reshape(2 * t, t)
        _softmax_step(s + block_mask(qi - 1), v, m_ref, l_ref, acc_ref)

    def far_step(j, carry):
        k, v = key_tile(j)
        _softmax_step(_qk(q2, k) + block_mask(j), v, m_ref, l_ref, acc_ref)
        return carry

    lax.fori_loop(0, jnp.maximum(qi - 1, 0), far_step, 0)

    o = acc_ref[...] / l_ref[...]
    lane_o = lax.broadcasted_iota(jnp.int32, (t, LANES), 1)
    o_ref[...] = jnp.where(lane_o < half, o[:t], o[t:]).astype(o_ref.dtype)


def _moba_attn(rel_t, qkv, neg, batch, seq):
    t = MOBA_TILE
    nq = seq // t
    n_pairs = MOBA_HEADS // 2
    grid_spec = pltpu.PrefetchScalarGridSpec(
        num_scalar_prefetch=1,
        grid=(batch, n_pairs, nq),
        in_specs=[
            pl.BlockSpec((t, LANES), lambda b, p, i, rel: (b * nq + i, COL_MQ // LANES + p)),
            pl.BlockSpec((seq, LANES), lambda b, p, i, rel: (b, COL_MK // LANES + p)),
            pl.BlockSpec((seq, LANES), lambda b, p, i, rel: (b, COL_MV // LANES + p)),
            pl.BlockSpec((t, LANES), lambda b, p, i, rel: (b * nq + i, p)),
        ],
        out_specs=pl.BlockSpec((t, LANES), lambda b, p, i, rel: (b * nq + i, p)),
        scratch_shapes=[
            pltpu.VMEM((2, t, 2 * t), jnp.float32),
            pltpu.VMEM((2 * t, LANES), jnp.float32),
            pltpu.VMEM((2 * t, 1), jnp.float32),
            pltpu.VMEM((2 * t, 1), jnp.float32),
            pltpu.VMEM((2 * t, LANES), jnp.float32),
        ],
    )
    return pl.pallas_call(
        _moba_attn_kernel,
        grid_spec=grid_spec,
        out_shape=jax.ShapeDtypeStruct((batch * seq, MOBA_HEADS * HEAD_DIM), jnp.bfloat16),
        compiler_params=_compiler_params(3),
        name="moba_attn",
    )(rel_t, qkv, qkv, qkv, neg)


def _merge_kernel(ad_ref, am_ref, gl_ref, bg_ref, h_ref, wbd_ref, wbm_ref, wo_ref,
                  g_ref, b_ref, o_ref):
    y_d = jnp.dot(ad_ref[...], wbd_ref[...], preferred_element_type=jnp.float32)
    y_m = jnp.dot(am_ref[...], wbm_ref[...], preferred_element_type=jnp.float32)
    gates = jax.nn.sigmoid(gl_ref[...] + bg_ref[...])
    mixed = gates[:, :D_MODEL] * y_d + gates[:, D_MODEL:] * y_m
    out = jnp.dot(mixed.astype(jnp.bfloat16), wo_ref[...], preferred_element_type=jnp.float32)
    o_ref[...] = _layer_norm(DEEPNORM_ALPHA * h_ref[...] + out, g_ref[...], b_ref[...])


def _merge(attn_d, attn_m, gl, b_gate, h, wbd, wbm, wout, ln_g, ln_b):
    n_rows = h.shape[0]
    tm = ROW_TILE
    row = lambda cols: pl.BlockSpec((tm, cols), lambda i: (i, 0))
    return pl.pallas_call(
        _merge_kernel,
        grid=(n_rows // tm,),
        in_specs=[row(D_MODEL), row(MOBA_HEADS * HEAD_DIM), row(2 * D_MODEL),
                  _resident((1, 2 * D_MODEL)), row(D_MODEL),
                  _resident(wbd.shape), _resident(wbm.shape), _resident(wout.shape),
                  _resident((1, D_MODEL)), _resident((1, D_MODEL))],
        out_specs=row(D_MODEL),
        out_shape=jax.ShapeDtypeStruct((n_rows, D_MODEL), jnp.float32),
        compiler_params=_compiler_params(1),
        name="merge",
    )(attn_d, attn_m, gl, b_gate, h, wbd, wbm, wout, ln_g, ln_b)


def _xattn_kv_kernel(mem_ref, wk_ref, wv_ref, k_ref, v_ref):
    mb = mem_ref[...].astype(jnp.bfloat16)
    k_ref[...] = jnp.dot(mb, wk_ref[...], preferred_element_type=jnp.float32).astype(k_ref.dtype)
    v_ref[...] = jnp.dot(mb, wv_ref[...], preferred_element_type=jnp.float32).astype(v_ref.dtype)


def _xattn_kv(mem2, wk, wv, batch, n_mem):
    blk = pl.BlockSpec((n_mem, D_MODEL), lambda b: (b, 0))
    return pl.pallas_call(
        _xattn_kv_kernel,
        grid=(batch,),
        in_specs=[blk, _resident(wk.shape), _resident(wv.shape)],
        out_specs=[blk, blk],
        out_shape=[jax.ShapeDtypeStruct(mem2.shape, jnp.bfloat16)] * 2,
        compiler_params=_compiler_params(1),
        name="xattn_kv",
    )(mem2, wk, wv)


def _xattn_kernel(h_ref, wq_ref, k_ref, v_ref, wo_ref, g_ref, b_ref, o_ref):
    h = h_ref[...]
    q = jnp.dot(h.astype(jnp.bfloat16), wq_ref[...], preferred_element_type=jnp.float32)
    q = (q * XATTN_HEAD_DIM ** -0.5).astype(jnp.bfloat16)
    outs = []
    for hd in range(XATTN_HEADS):
        cols = slice(hd * XATTN_HEAD_DIM, (hd + 1) * XATTN_HEAD_DIM)
        s = _qk(q[:, cols], k_ref[:, cols])
        p = jnp.exp(s - jnp.max(s, axis=1, keepdims=True))
        p = p / jnp.sum(p, axis=1, keepdims=True)
        outs.append(jnp.dot(p.astype(jnp.bfloat16), v_ref[:, cols],
                            preferred_element_type=jnp.float32))
    o = jnp.concatenate(outs, axis=1).astype(jnp.bfloat16)
    xa = jnp.dot(o, wo_ref[...], preferred_element_type=jnp.float32)
    o_ref[...] = _layer_norm(DEEPNORM_ALPHA * h + xa, g_ref[...], b_ref[...])


def _xattn(h1, wq, kx, vx, wo, ln_g, ln_b, batch, seq, n_mem):
    tm = ROW_TILE
    nq = seq // tm
    row = pl.BlockSpec((tm, D_MODEL), lambda b, i: (b * nq + i, 0))
    kv = pl.BlockSpec((n_mem, D_MODEL), lambda b, i: (b, 0))
    return pl.pallas_call(
        _xattn_kernel,
        grid=(batch, nq),
        in_specs=[row, _resident(wq.shape), kv, kv, _resident(wo.shape),
                  _resident((1, D_MODEL)), _resident((1, D_MODEL))],
        out_specs=row,
        out_shape=jax.ShapeDtypeStruct(h1.shape, jnp.float32),
        compiler_params=_compiler_params(2),
        name="xattn",
    )(h1, wq, kx, vx, wo, ln_g, ln_b)


def _mlp_kernel(h_ref, w1_ref, w2_ref, g_ref, b_ref, o_ref, hb_ref, acc_ref):
    j = pl.program_id(1)

    @pl.when(j == 0)
    def _():
        hb_ref[...] = h_ref[...].astype(jnp.bfloat16)
        acc_ref[...] = jnp.zeros(acc_ref.shape, jnp.float32)

    u = jnp.dot(hb_ref[...], w1_ref[...], preferred_element_type=jnp.float32)
    u = jnp.square(jnp.maximum(u, 0.0)).astype(jnp.bfloat16)
    acc_ref[...] += jnp.dot(u, w2_ref[...], preferred_element_type=jnp.float32)

    @pl.when(j == pl.num_programs(1) - 1)
    def _():
        o_ref[...] = _layer_norm(DEEPNORM_ALPHA * h_ref[...] + acc_ref[...],
                                 g_ref[...], b_ref[...])


def _mlp(h2, w1, w2, ln_g, ln_b):
    n_rows = h2.shape[0]
    tm = ROW_TILE
    row = pl.BlockSpec((tm, D_MODEL), lambda i, j: (i, 0))
    return pl.pallas_call(
        _mlp_kernel,
        grid=(n_rows // tm, D_FF // FF_TILE),
        in_specs=[row,
                  pl.BlockSpec((D_MODEL, FF_TILE), lambda i, j: (0, j)),
                  pl.BlockSpec((FF_TILE, D_MODEL), lambda i, j: (j, 0)),
                  _resident((1, D_MODEL)), _resident((1, D_MODEL))],
        out_specs=row,
        out_shape=jax.ShapeDtypeStruct(h2.shape, jnp.float32),
        scratch_shapes=[pltpu.VMEM((tm, D_MODEL), jnp.bfloat16),
                        pltpu.VMEM((tm, D_MODEL), jnp.float32)],
        compiler_params=_compiler_params(2),
        name="mlp",
    )(h2, w1, w2, ln_g, ln_b)


def kernel(x, mem, ln_in_g, ln_in_b, rel_table, w_in, b_gate, lam_q1, lam_k1, lam_q2, lam_k2,
           diff_sub_g, w_br_diff, w_br_moba, w_out, ln1_g, ln1_b, wq_x, wk_x, wv_x, wo_x,
           ln2_g, ln2_b, w_ff1, w_ff2, ln3_g, ln3_b):
    batch, seq, d = x.shape
    n_mem = mem.shape[1]
    assert d == D_MODEL and w_in.shape == (DEPTH, D_MODEL, W_IN_COLS)
    assert seq % ROW_TILE == 0 and seq % DIFF_TILE == 0 and seq % SEL_TILE == 0
    assert seq // MOBA_BLOCK <= LANES // 2 and MOBA_TILE >= REL_FAR_DIST <= DIFF_TILE
    bf = lambda w: w.astype(jnp.bfloat16)
    vec = lambda v: v.reshape(1, -1).astype(jnp.float32)
    l = 0

    x2 = x.reshape(batch * seq, d)
    h, qkv, mqf, gl, kmean = _ln_inproj(x2, vec(ln_in_g), vec(ln_in_b), bf(w_in[l]))

    n_blocks = seq // MOBA_BLOCK
    km = kmean.reshape(batch, n_blocks, MOBA_HEADS // 2, 2, HEAD_DIM)
    km = jnp.pad(km, ((0, 0), (0, LANES // 2 - n_blocks), (0, 0), (0, 0), (0, 0)))
    eye2 = jnp.eye(2, dtype=jnp.float32)
    km2 = jnp.einsum('bnpad,ac->bpadcn', km, eye2).reshape(batch, MOBA_HEADS // 2, LANES, LANES)
    neg = _moba_select(mqf, km2, batch, seq)

    rel_t = rel_table.T.astype(jnp.float32)
    lam_p = jnp.concatenate([lam_q1[l:l + 1], lam_k1[l:l + 1], lam_q2[l:l + 1], lam_k2[l:l + 1]],
                            axis=0).astype(jnp.float32)
    attn_d = _diff_attn(rel_t, qkv, lam_p, vec(diff_sub_g[l]), batch, seq)
    attn_m = _moba_attn(rel_t, qkv, neg, batch, seq)

    h1 = _merge(attn_d, attn_m, gl, vec(b_gate[l]), h, bf(w_br_diff[l]), bf(w_br_moba[l]),
                bf(w_out[l]), vec(ln1_g[l]), vec(ln1_b[l]))

    kx, vx = _xattn_kv(mem.reshape(batch * n_mem, d), bf(wk_x[l]), bf(wv_x[l]), batch, n_mem)
    h2 = _xattn(h1, bf(wq_x[l]), kx, vx, bf(wo_x[l]), vec(ln2_g[l]), vec(ln2_b[l]),
                batch, seq, n_mem)

    h3 = _mlp(h2, bf(w_ff1[l]), bf(w_ff2[l]), vec(ln3_g[l]), vec(ln3_b[l]))
    return h3.reshape(batch, seq, d)
```

```python
import functools
import math

import numpy as np
import jax
import jax.numpy as jnp
from jax import lax
from jax.experimental import pallas as pl
from jax.experimental.pallas import tpu as pltpu

D_MODEL = 1024
HEAD_DIM = 64
DIFF_HEADS = 8
MOBA_HEADS = 8
MOBA_BLOCK = 256
MOBA_TOPK = 3
XATTN_HEADS = 4
XATTN_HEAD_DIM = D_MODEL // XATTN_HEADS
D_FF = 4 * D_MODEL
REL_BUCKETS = 32
REL_MAX_DIST = 128
LN_EPS = 1e-5
DEPTH = 1
DEEPNORM_ALPHA = (2.0 * DEPTH) ** 0.25
LAM_INIT = 0.8 - 0.6 * math.exp(-0.3 * 0)

COL_DQ = 0
COL_DK = 1024
COL_DV = 2048
COL_MQ = 3072
COL_MK = 3584
COL_MV = 4096
COL_GATE = 4608
N_QKV_COLS = 4608
W_IN_COLS = 6656

LANES = 128
VMEM_LIMIT_BYTES = 56 * 1024 * 1024

ROW_TILE = 512
DIFF_TILE = 512
MOBA_TILE = 512
SEL_TILE = 512
FF_TILE = 1024
PROJ_CHUNK = 512

MASK_NEG = -1e30


def _rel_thresholds():
    max_exact = REL_BUCKETS // 2
    n = np.arange(0, 4 * REL_MAX_DIST)
    nf = np.maximum(n, 1).astype(np.float32)
    large = max_exact + (np.log(nf / max_exact) / math.log(REL_MAX_DIST / max_exact)
                         * (REL_BUCKETS - max_exact)).astype(np.int32)
    large = np.minimum(large, REL_BUCKETS - 1)
    bucket = np.where(n < max_exact, n, large)
    thr = [int(np.argmax(bucket >= k)) for k in range(REL_BUCKETS)]
    assert all(bucket[t] == k for k, t in enumerate(thr))
    return thr


REL_THRESHOLDS = _rel_thresholds()
REL_FAR_DIST = REL_THRESHOLDS[-1]


def _layer_norm(z, g, b):
    mu = jnp.mean(z, axis=-1, keepdims=True)
    zc = z - mu
    var = jnp.mean(zc * zc, axis=-1, keepdims=True)
    return zc * lax.rsqrt(var + LN_EPS) * g + b


def _compiler_params(n_grid):
    return pltpu.CompilerParams(
        dimension_semantics=("arbitrary",) * n_grid,
        vmem_limit_bytes=VMEM_LIMIT_BYTES,
    )


def _resident(shape):
    nd = len(shape)
    return pl.BlockSpec(shape, lambda *_: (0,) * nd, pipeline_mode=pl.Buffered(1))


def _ln_inproj_kernel(x_ref, g_ref, b_ref, w_ref, h_ref, qkv_ref, mqf_ref, gl_ref, kmean_ref):
    h = _layer_norm(x_ref[...], g_ref[...], b_ref[...])
    h_ref[...] = h
    hb = h.astype(jnp.bfloat16)
    q_scale = HEAD_DIM ** -0.5
    for c0 in range(0, W_IN_COLS, PROJ_CHUNK):
        acc = jnp.dot(hb, w_ref[:, c0:c0 + PROJ_CHUNK], preferred_element_type=jnp.float32)
        if c0 < COL_GATE:
            is_q = c0 < COL_DK or COL_MQ <= c0 < COL_MK
            if COL_MQ <= c0 < COL_MK:
                mqf_ref[:, c0 - COL_MQ:c0 - COL_MQ + PROJ_CHUNK] = acc
            if COL_MK <= c0 < COL_MV:
                nblk = acc.shape[0] // MOBA_BLOCK
                km = jnp.mean(acc.reshape(nblk, MOBA_BLOCK, PROJ_CHUNK), axis=1)
                kmean_ref[0, :, c0 - COL_MK:c0 - COL_MK + PROJ_CHUNK] = km
            val = acc * q_scale if is_q else acc
            qkv_ref[:, c0:c0 + PROJ_CHUNK] = val.astype(jnp.bfloat16)
        else:
            gl_ref[:, c0 - COL_GATE:c0 - COL_GATE + PROJ_CHUNK] = acc


def _ln_inproj(x2, ln_g, ln_b, w_in_bf):
    n_rows = x2.shape[0]
    tm = ROW_TILE
    nblk = tm // MOBA_BLOCK
    grid = (n_rows // tm,)
    row = lambda cols: pl.BlockSpec((tm, cols), lambda i: (i, 0))
    return pl.pallas_call(
        _ln_inproj_kernel,
        grid=grid,
        in_specs=[row(D_MODEL), _resident((1, D_MODEL)), _resident((1, D_MODEL)),
                  _resident((D_MODEL, W_IN_COLS))],
        out_specs=[row(D_MODEL), row(N_QKV_COLS), row(MOBA_HEADS * HEAD_DIM), row(2 * D_MODEL),
                   pl.BlockSpec((1, nblk, MOBA_HEADS * HEAD_DIM), lambda i: (i, 0, 0))],
        out_shape=[
            jax.ShapeDtypeStruct((n_rows, D_MODEL), jnp.float32),
            jax.ShapeDtypeStruct((n_rows, N_QKV_COLS), jnp.bfloat16),
            jax.ShapeDtypeStruct((n_rows, MOBA_HEADS * HEAD_DIM), jnp.float32),
            jax.ShapeDtypeStruct((n_rows, 2 * D_MODEL), jnp.float32),
            jax.ShapeDtypeStruct((n_rows // tm, nblk, MOBA_HEADS * HEAD_DIM), jnp.float32),
        ],
        compiler_params=_compiler_params(1),
        name="ln_inproj",
    )(x2, ln_g, ln_b, w_in_bf)


def _moba_select_kernel(q_ref, km_ref, neg_ref, *, n_blocks):
    t = q_ref.shape[0]
    half = LANES // 2
    row0 = pl.program_id(1) * t
    lane = lax.broadcasted_iota(jnp.int32, (t, LANES), 1)
    blk = jnp.where(lane < half, lane, lane - half)
    cur = (row0 + lax.broadcasted_iota(jnp.int32, (t, LANES), 0)) // MOBA_BLOCK
    first = lane < half
    ninf = jnp.float32(-jnp.inf)
    for pair in range(MOBA_HEADS // 2):
        q = q_ref[:, pair * LANES:(pair + 1) * LANES]
        gate = jnp.dot(q, km_ref[0, pair], preferred_element_type=jnp.float32,
                       precision=lax.Precision.HIGHEST)
        g = jnp.where((blk < cur) & (blk < n_blocks), gate, ninf)
        neg = jnp.full((t, LANES), MASK_NEG, jnp.float32)
        for rank in range(MOBA_TOPK):
            mx_a = jnp.max(jnp.where(first, g, ninf), axis=1, keepdims=True)
            mx_b = jnp.max(jnp.where(first, ninf, g), axis=1, keepdims=True)
            mx = jnp.where(first, mx_a, mx_b)
            cand = jnp.where(g == mx, blk, half)
            ix_a = jnp.min(jnp.where(first, cand, half), axis=1, keepdims=True)
            ix_b = jnp.min(jnp.where(first, half, cand), axis=1, keepdims=True)
            pick = blk == jnp.where(first, ix_a, ix_b)
            neg = jnp.where(pick & (rank < cur), 0.0, neg)
            g = jnp.where(pick, ninf, g)
        neg_ref[:, pair * LANES:(pair + 1) * LANES] = neg


def _moba_select(mqf, km2, batch, seq):
    t = SEL_TILE
    nq = seq // t
    n_blocks = seq // MOBA_BLOCK
    return pl.pallas_call(
        functools.partial(_moba_select_kernel, n_blocks=n_blocks),
        grid=(batch, nq),
        in_specs=[pl.BlockSpec((t, MOBA_HEADS * HEAD_DIM), lambda b, i: (b * nq + i, 0)),
                  pl.BlockSpec((1, MOBA_HEADS // 2, LANES, LANES), lambda b, i: (b, 0, 0, 0))],
        out_specs=pl.BlockSpec((t, MOBA_HEADS * HEAD_DIM), lambda b, i: (b * nq + i, 0)),
        out_shape=jax.ShapeDtypeStruct(mqf.shape, jnp.float32),
        compiler_params=_compiler_params(2),
        name="moba_select",
    )(mqf, km2)


def _fill_bias_table(tab_ref, rel_ref, head, t):
    far = rel_ref[head, REL_BUCKETS - 1]
    r = lax.broadcasted_iota(jnp.int32, (t, 2 * t), 0)
    c = lax.broadcasted_iota(jnp.int32, (t, 2 * t), 1)
    dist = r - c + t
    bias = jnp.full((t, 2 * t), rel_ref[head, 0] - far, jnp.float32)
    for k in range(1, REL_BUCKETS):
        bias = jnp.where(dist >= REL_THRESHOLDS[k], rel_ref[head, k] - far, bias)
    tab_ref[...] = jnp.where(dist >= 0, bias, MASK_NEG)


def _softmax_step(s, v_aug, m_ref, acc_ref):
    m_prev = m_ref[...]
    m_new = jnp.maximum(m_prev, jnp.max(s, axis=1, keepdims=True))
    alpha = jnp.exp(m_prev - m_new)
    p = jnp.exp(s - jnp.tile(m_new, (1, s.shape[1] // LANES)))
    pv = jnp.dot(p.astype(jnp.bfloat16), v_aug, preferred_element_type=jnp.float32)
    acc_ref[...] = jnp.tile(alpha, (1, 2)) * acc_ref[...] + pv
    m_ref[...] = m_new


def _far_tiles(n, logits_of, values_of, sa_ref, sb_ref, m_ref, acc_ref):
    def pair(i, carry):
        j = 2 * i
        sb_ref[...] = logits_of(j + 1)
        _softmax_step(sa_ref[...], values_of(j), m_ref, acc_ref)
        sa_ref[...] = logits_of(jnp.minimum(j + 2, n - 1))
        _softmax_step(sb_ref[...], values_of(j + 1), m_ref, acc_ref)
        return carry

    lax.fori_loop(0, lax.shift_right_logical(n, 1), pair, 0)

    @pl.when((n & 1) == 1)
    def _():
        _softmax_step(sa_ref[...], values_of(n - 1), m_ref, acc_ref)


def _init_softmax(m_ref, acc_ref):
    m_ref[...] = jnp.full(m_ref.shape, MASK_NEG, jnp.float32)
    acc_ref[...] = jnp.zeros(acc_ref.shape, jnp.float32)


def _fill_values_aug(vaug_ref, v_ref):
    vaug_ref[:, 0:LANES] = v_ref[...]
    vaug_ref[:, LANES:2 * LANES] = jnp.ones(v_ref.shape, vaug_ref.dtype)


def _qk(q2, k):
    return lax.dot_general(q2, k, (((1,), (1,)), ((), ())), preferred_element_type=jnp.float32)


def _split_halves(q):
    lane = lax.broadcasted_iota(jnp.int32, q.shape, 1)
    zero = jnp.zeros_like(q)
    return jnp.concatenate([jnp.where(lane < LANES // 2, q, zero),
                            jnp.where(lane < LANES // 2, zero, q)], axis=0)


def _diff_attn_kernel(rel_ref, q_ref, k_ref, v_ref, lam_ref, subg_ref, o_ref,
                      tab_ref, vaug_ref, sa_ref, sb_ref, m_ref, acc_ref):
    t = q_ref.shape[0]
    head = pl.program_id(1)
    qi = pl.program_id(2)

    @pl.when(qi == 0)
    def _():
        _fill_bias_table(tab_ref, rel_ref, head, t)
        _fill_values_aug(vaug_ref, v_ref)

    q2 = _split_halves(q_ref[...])

    def logits_of(j):
        return _qk(q2, k_ref[pl.ds(pl.multiple_of(j * t, t), t), :])

    def values_of(j):
        return vaug_ref[pl.ds(pl.multiple_of(j * t, t), t), :]

    def near_step(j, col0):
        bias = tab_ref[:, col0:col0 + t]
        s = (logits_of(j).reshape(2, t, t) + bias[None]).reshape(2 * t, t)
        _softmax_step(s, values_of(j), m_ref, acc_ref)

    _init_softmax(m_ref, acc_ref)
    sa_ref[...] = logits_of(0)
    near_step(qi, t)

    @pl.when(qi >= 1)
    def _():
        near_step(qi - 1, 0)

    _far_tiles(jnp.maximum(qi - 1, 0), logits_of, values_of, sa_ref, sb_ref, m_ref, acc_ref)

    lam_p = lam_ref[...]
    lam = (jnp.exp(jnp.sum(lam_p[0:1] * lam_p[1:2], axis=1, keepdims=True))
           - jnp.exp(jnp.sum(lam_p[2:3] * lam_p[3:4], axis=1, keepdims=True)) + LAM_INIT)
    o = acc_ref[:, 0:LANES] / acc_ref[:, LANES:2 * LANES]
    o = o[:t] - lam * o[t:]
    o = o * lax.rsqrt(jnp.mean(o * o, axis=1, keepdims=True) + LN_EPS) * subg_ref[...]
    o_ref[...] = (o * (1.0 - LAM_INIT)).astype(o_ref.dtype)


def _diff_attn(rel_t, qkv, lam_p, sub_g, batch, seq):
    t = DIFF_TILE
    nq = seq // t
    grid_spec = pltpu.PrefetchScalarGridSpec(
        num_scalar_prefetch=1,
        grid=(batch, DIFF_HEADS, nq),
        in_specs=[
            pl.BlockSpec((t, LANES), lambda b, h, i, rel: (b * nq + i, COL_DQ // LANES + h)),
            pl.BlockSpec((seq, LANES), lambda b, h, i, rel: (b, COL_DK // LANES + h)),
            pl.BlockSpec((seq, LANES), lambda b, h, i, rel: (b, COL_DV // LANES + h)),
            pl.BlockSpec((4, HEAD_DIM), lambda b, h, i, rel: (0, 0)),
            pl.BlockSpec((1, LANES), lambda b, h, i, rel: (0, 0)),
        ],
        out_specs=pl.BlockSpec((t, LANES), lambda b, h, i, rel: (b * nq + i, h)),
        scratch_shapes=[
            pltpu.VMEM((t, 2 * t), jnp.float32),
            pltpu.VMEM((seq, 2 * LANES), jnp.bfloat16),
            pltpu.VMEM((2 * t, t), jnp.float32),
            pltpu.VMEM((2 * t, t), jnp.float32),
            pltpu.VMEM((2 * t, LANES), jnp.float32),
            pltpu.VMEM((2 * t, 2 * LANES), jnp.float32),
        ],
    )
    return pl.pallas_call(
        _diff_attn_kernel,
        grid_spec=grid_spec,
        out_shape=jax.ShapeDtypeStruct((batch * seq, DIFF_HEADS * LANES), jnp.bfloat16),
        compiler_params=_compiler_params(3),
        name="diff_attn",
    )(rel_t, qkv, qkv, qkv, lam_p, sub_g)


def _moba_attn_kernel(rel_ref, q_ref, k_ref, v_ref, neg_ref, o_ref,
                      tab_ref, kaug_ref, vaug_ref, lhs_ref, sa_ref, sb_ref, m_ref, acc_ref):
    t = q_ref.shape[0]
    seq = k_ref.shape[0]
    half = LANES // 2
    blocks_per_tile = t // MOBA_BLOCK
    pair = pl.program_id(1)
    qi = pl.program_id(2)

    @pl.when(qi == 0)
    def _():
        _fill_bias_table(tab_ref.at[0], rel_ref, DIFF_HEADS + 2 * pair, t)
        _fill_bias_table(tab_ref.at[1], rel_ref, DIFF_HEADS + 2 * pair + 1, t)
        _fill_values_aug(vaug_ref, v_ref)
        kaug_ref[:, 0:LANES] = k_ref[...]
        key_blk = lax.broadcasted_iota(jnp.int32, (seq, LANES), 0) // MOBA_BLOCK
        lane_k = lax.broadcasted_iota(jnp.int32, (seq, LANES), 1)
        kaug_ref[:, LANES:2 * LANES] = jnp.where(key_blk == lane_k, 1.0, 0.0).astype(kaug_ref.dtype)

    q2 = _split_halves(q_ref[...])
    neg = neg_ref[...]
    neg2 = jnp.concatenate([neg, pltpu.roll(neg, half, axis=1)], axis=0)
    lhs_ref[:, 0:LANES] = q2
    lhs_ref[:, LANES:2 * LANES] = neg2.astype(lhs_ref.dtype)

    def keys_of(j):
        return kaug_ref[pl.ds(pl.multiple_of(j * t, t), t), :]

    def logits_of(j):
        return _qk(lhs_ref[...], keys_of(j))

    def values_of(j):
        return vaug_ref[pl.ds(pl.multiple_of(j * t, t), t), :]

    _init_softmax(m_ref, acc_ref)
    sa_ref[...] = logits_of(0)

    s = _qk(q2, keys_of(qi)[:, 0:LANES]).reshape(2, t, t) + tab_ref[:, :, t:2 * t]
    row_blk = lax.broadcasted_iota(jnp.int32, (2, t, t), 1) // MOBA_BLOCK
    col_blk = lax.broadcasted_iota(jnp.int32, (2, t, t), 2) // MOBA_BLOCK
    lane = lax.broadcasted_iota(jnp.int32, (2 * t, LANES), 1)
    for cb in range(blocks_per_tile - 1):
        chosen = jnp.sum(jnp.where(lane == qi * blocks_per_tile + cb, neg2, 0.0),
                         axis=1, keepdims=True).reshape(2, t, 1)
        s = s + jnp.where((col_blk == cb) & (row_blk > cb), chosen, 0.0)
    _softmax_step(s.reshape(2 * t, t), values_of(qi), m_ref, acc_ref)

    @pl.when(qi >= 1)
    def _():
        s = logits_of(qi - 1).reshape(2, t, t) + tab_ref[:, :, 0:t]
        _softmax_step(s.reshape(2 * t, t), values_of(qi - 1), m_ref, acc_ref)

    _far_tiles(jnp.maximum(qi - 1, 0), logits_of, values_of, sa_ref, sb_ref, m_ref, acc_ref)

    o = acc_ref[:, 0:LANES] / acc_ref[:, LANES:2 * LANES]
    lane_o = lax.broadcasted_iota(jnp.int32, (t, LANES), 1)
    o_ref[...] = jnp.where(lane_o < half, o[:t], o[t:]).astype(o_ref.dtype)


def _moba_attn(rel_t, qkv, neg, batch, seq):
    t = MOBA_TILE
    nq = seq // t
    n_pairs = MOBA_HEADS // 2
    grid_spec = pltpu.PrefetchScalarGridSpec(
        num_scalar_prefetch=1,
        grid=(batch, n_pairs, nq),
        in_specs=[
            pl.BlockSpec((t, LANES), lambda b, p, i, rel: (b * nq + i, COL_MQ // LANES + p)),
            pl.BlockSpec((seq, LANES), lambda b, p, i, rel: (b, COL_MK // LANES + p)),
            pl.BlockSpec((seq, LANES), lambda b, p, i, rel: (b, COL_MV // LANES + p)),
            pl.BlockSpec((t, LANES), lambda b, p, i, rel: (b * nq + i, p)),
        ],
        out_specs=pl.BlockSpec((t, LANES), lambda b, p, i, rel: (b * nq + i, p)),
        scratch_shapes=[
            pltpu.VMEM((2, t, 2 * t), jnp.float32),
            pltpu.VMEM((seq, 2 * LANES), jnp.bfloat16),
            pltpu.VMEM((seq, 2 * LANES), jnp.bfloat16),
            pltpu.VMEM((2 * t, 2 * LANES), jnp.bfloat16),
            pltpu.VMEM((2 * t, t), jnp.float32),
            pltpu.VMEM((2 * t, t), jnp.float32),
            pltpu.VMEM((2 * t, LANES), jnp.float32),
            pltpu.VMEM((2 * t, 2 * LANES), jnp.float32),
        ],
    )
    return pl.pallas_call(
        _moba_attn_kernel,
        grid_spec=grid_spec,
        out_shape=jax.ShapeDtypeStruct((batch * seq, MOBA_HEADS * HEAD_DIM), jnp.bfloat16),
        compiler_params=_compiler_params(3),
        name="moba_attn",
    )(rel_t, qkv, qkv, qkv, neg)


def _merge_kernel(ad_ref, am_ref, gl_ref, bg_ref, h_ref, wbd_ref, wbm_ref, wo_ref,
                  g_ref, b_ref, o_ref):
    y_d = jnp.dot(ad_ref[...], wbd_ref[...], preferred_element_type=jnp.float32)
    y_m = jnp.dot(am_ref[...], wbm_ref[...], preferred_element_type=jnp.float32)
    gates = jax.nn.sigmoid(gl_ref[...] + bg_ref[...])
    mixed = gates[:, :D_MODEL] * y_d + gates[:, D_MODEL:] * y_m
    out = jnp.dot(mixed.astype(jnp.bfloat16), wo_ref[...], preferred_element_type=jnp.float32)
    o_ref[...] = _layer_norm(DEEPNORM_ALPHA * h_ref[...] + out, g_ref[...], b_ref[...])


def _merge(attn_d, attn_m, gl, b_gate, h, wbd, wbm, wout, ln_g, ln_b):
    n_rows = h.shape[0]
    tm = ROW_TILE
    row = lambda cols: pl.BlockSpec((tm, cols), lambda i: (i, 0))
    return pl.pallas_call(
        _merge_kernel,
        grid=(n_rows // tm,),
        in_specs=[row(D_MODEL), row(MOBA_HEADS * HEAD_DIM), row(2 * D_MODEL),
                  _resident((1, 2 * D_MODEL)), row(D_MODEL),
                  _resident(wbd.shape), _resident(wbm.shape), _resident(wout.shape),
                  _resident((1, D_MODEL)), _resident((1, D_MODEL))],
        out_specs=row(D_MODEL),
        out_shape=jax.ShapeDtypeStruct((n_rows, D_MODEL), jnp.float32),
        compiler_params=_compiler_params(1),
        name="merge",
    )(attn_d, attn_m, gl, b_gate, h, wbd, wbm, wout, ln_g, ln_b)


def _xattn_kv_kernel(mem_ref, wk_ref, wv_ref, k_ref, v_ref):
    mb = mem_ref[...].astype(jnp.bfloat16)
    k_ref[...] = jnp.dot(mb, wk_ref[...], preferred_element_type=jnp.float32).astype(k_ref.dtype)
    v_ref[...] = jnp.dot(mb, wv_ref[...], preferred_element_type=jnp.float32).astype(v_ref.dtype)


def _xattn_kv(mem2, wk, wv, batch, n_mem):
    blk = pl.BlockSpec((n_mem, D_MODEL), lambda b: (b, 0))
    return pl.pallas_call(
        _xattn_kv_kernel,
        grid=(batch,),
        in_specs=[blk, _resident(wk.shape), _resident(wv.shape)],
        out_specs=[blk, blk],
        out_shape=[jax.ShapeDtypeStruct(mem2.shape, jnp.bfloat16)] * 2,
        compiler_params=_compiler_params(1),
        name="xattn_kv",
    )(mem2, wk, wv)


def _xattn_kernel(h_ref, wq_ref, k_ref, v_ref, wo_ref, g_ref, b_ref, o_ref):
    h = h_ref[...]
    q = jnp.dot(h.astype(jnp.bfloat16), wq_ref[...], preferred_element_type=jnp.float32)
    q = (q * XATTN_HEAD_DIM ** -0.5).astype(jnp.bfloat16)
    outs = []
    for hd in range(XATTN_HEADS):
        cols = slice(hd * XATTN_HEAD_DIM, (hd + 1) * XATTN_HEAD_DIM)
        s = _qk(q[:, cols], k_ref[:, cols])
        p = jnp.exp(s - jnp.max(s, axis=1, keepdims=True))
        p = p / jnp.sum(p, axis=1, keepdims=True)
        outs.append(jnp.dot(p.astype(jnp.bfloat16), v_ref[:, cols],
                            preferred_element_type=jnp.float32))
    o = jnp.concatenate(outs, axis=1).astype(jnp.bfloat16)
    xa = jnp.dot(o, wo_ref[...], preferred_element_type=jnp.float32)
    o_ref[...] = _layer_norm(DEEPNORM_ALPHA * h + xa, g_ref[...], b_ref[...])


def _xattn(h1, wq, kx, vx, wo, ln_g, ln_b, batch, seq, n_mem):
    tm = ROW_TILE
    nq = seq // tm
    row = pl.BlockSpec((tm, D_MODEL), lambda b, i: (b * nq + i, 0))
    kv = pl.BlockSpec((n_mem, D_MODEL), lambda b, i: (b, 0))
    return pl.pallas_call(
        _xattn_kernel,
        grid=(batch, nq),
        in_specs=[row, _resident(wq.shape), kv, kv, _resident(wo.shape),
                  _resident((1, D_MODEL)), _resident((1, D_MODEL))],
        out_specs=row,
        out_shape=jax.ShapeDtypeStruct(h1.shape, jnp.float32),
        compiler_params=_compiler_params(2),
        name="xattn",
    )(h1, wq, kx, vx, wo, ln_g, ln_b)


def _mlp_kernel(h_ref, w1_ref, w2_ref, g_ref, b_ref, o_ref, hb_ref, acc_ref):
    j = pl.program_id(1)

    @pl.when(j == 0)
    def _():
        hb_ref[...] = h_ref[...].astype(jnp.bfloat16)
        acc_ref[...] = jnp.zeros(acc_ref.shape, jnp.float32)

    u = jnp.dot(hb_ref[...], w1_ref[...], preferred_element_type=jnp.float32)
    u = jnp.square(jnp.maximum(u, 0.0)).astype(jnp.bfloat16)
    acc_ref[...] += jnp.dot(u, w2_ref[...], preferred_element_type=jnp.float32)

    @pl.when(j == pl.num_programs(1) - 1)
    def _():
        o_ref[...] = _layer_norm(DEEPNORM_ALPHA * h_ref[...] + acc_ref[...],
                                 g_ref[...], b_ref[...])


def _mlp(h2, w1, w2, ln_g, ln_b):
    n_rows = h2.shape[0]
    tm = ROW_TILE
    row = pl.BlockSpec((tm, D_MODEL), lambda i, j: (i, 0))
    return pl.pallas_call(
        _mlp_kernel,
        grid=(n_rows // tm, D_FF // FF_TILE),
        in_specs=[row,
                  pl.BlockSpec((D_MODEL, FF_TILE), lambda i, j: (0, j)),
                  pl.BlockSpec((FF_TILE, D_MODEL), lambda i, j: (j, 0)),
                  _resident((1, D_MODEL)), _resident((1, D_MODEL))],
        out_specs=row,
        out_shape=jax.ShapeDtypeStruct(h2.shape, jnp.float32),
        scratch_shapes=[pltpu.VMEM((tm, D_MODEL), jnp.bfloat16),
                        pltpu.VMEM((tm, D_MODEL), jnp.float32)],
        compiler_params=_compiler_params(2),
        name="mlp",
    )(h2, w1, w2, ln_g, ln_b)


def kernel(x, mem, ln_in_g, ln_in_b, rel_table, w_in, b_gate, lam_q1, lam_k1, lam_q2, lam_k2,
           diff_sub_g, w_br_diff, w_br_moba, w_out, ln1_g, ln1_b, wq_x, wk_x, wv_x, wo_x,
           ln2_g, ln2_b, w_ff1, w_ff2, ln3_g, ln3_b):
    batch, seq, d = x.shape
    n_mem = mem.shape[1]
    assert d == D_MODEL and w_in.shape == (DEPTH, D_MODEL, W_IN_COLS)
    assert seq % ROW_TILE == 0 and seq % DIFF_TILE == 0 and seq % SEL_TILE == 0
    assert seq % MOBA_TILE == 0 and MOBA_TILE % MOBA_BLOCK == 0
    assert seq // MOBA_BLOCK <= LANES // 2 and MOBA_TILE >= REL_FAR_DIST <= DIFF_TILE
    bf = lambda w: w.astype(jnp.bfloat16)
    vec = lambda v: v.reshape(1, -1).astype(jnp.float32)
    l = 0

    x2 = x.reshape(batch * seq, d)
    h, qkv, mqf, gl, kmean = _ln_inproj(x2, vec(ln_in_g), vec(ln_in_b), bf(w_in[l]))

    n_blocks = seq // MOBA_BLOCK
    km = kmean.reshape(batch, n_blocks, MOBA_HEADS // 2, 2, HEAD_DIM)
    km = jnp.pad(km, ((0, 0), (0, LANES // 2 - n_blocks), (0, 0), (0, 0), (0, 0)))
    eye2 = jnp.eye(2, dtype=jnp.float32)
    km2 = jnp.einsum('bnpad,ac->bpadcn', km, eye2).reshape(batch, MOBA_HEADS // 2, LANES, LANES)
    neg = _moba_select(mqf, km2, batch, seq)

    rel_t = rel_table.T.astype(jnp.float32)
    lam_p = jnp.concatenate([lam_q1[l:l + 1], lam_k1[l:l + 1], lam_q2[l:l + 1], lam_k2[l:l + 1]],
                            axis=0).astype(jnp.float32)
    attn_d = _diff_attn(rel_t, qkv, lam_p, vec(diff_sub_g[l]), batch, seq)
    attn_m = _moba_attn(rel_t, qkv, neg, batch, seq)

    h1 = _merge(attn_d, attn_m, gl, vec(b_gate[l]), h, bf(w_br_diff[l]), bf(w_br_moba[l]),
                bf(w_out[l]), vec(ln1_g[l]), vec(ln1_b[l]))

    kx, vx = _xattn_kv(mem.reshape(batch * n_mem, d), bf(wk_x[l]), bf(wv_x[l]), batch, n_mem)
    h2 = _xattn(h1, bf(wq_x[l]), kx, vx, bf(wo_x[l]), vec(ln2_g[l]), vec(ln2_b[l]),
                batch, seq, n_mem)

    h3 = _mlp(h2, bf(w_ff1[l]), bf(w_ff2[l]), vec(ln3_g[l]), vec(ln3_b[l]))
    return h3.reshape(batch, seq, d)
```

```python
import functools
import math

import numpy as np
import jax
import jax.numpy as jnp
from jax import lax
from jax.experimental import pallas as pl
from jax.experimental.pallas import tpu as pltpu

D_MODEL = 1024
HEAD_DIM = 64
DIFF_HEADS = 8
MOBA_HEADS = 8
MOBA_BLOCK = 256
MOBA_TOPK = 3
XATTN_HEADS = 4
XATTN_HEAD_DIM = D_MODEL // XATTN_HEADS
D_FF = 4 * D_MODEL
REL_BUCKETS = 32
REL_MAX_DIST = 128
LN_EPS = 1e-5
DEPTH = 1
DEEPNORM_ALPHA = (2.0 * DEPTH) ** 0.25
LAM_INIT = 0.8 - 0.6 * math.exp(-0.3 * 0)
LOG2_E = math.log2(math.e)

COL_DQ = 0
COL_DK = 1024
COL_DV = 2048
COL_MQ = 3072
COL_MK = 3584
COL_MV = 4096
COL_GATE = 4608
N_QKV_COLS = 4608
W_IN_COLS = 6656

LANES = 128
VMEM_LIMIT_BYTES = 56 * 1024 * 1024

ROW_TILE = 512
DIFF_TILE = 512
MOBA_TILE = 512
SEL_TILE = 512
FF_TILE = 1024
PROJ_CHUNK = 512
FAR_UNROLL = 4

MASK_NEG = -1e30


def _rel_thresholds():
    max_exact = REL_BUCKETS // 2
    n = np.arange(0, 4 * REL_MAX_DIST)
    nf = np.maximum(n, 1).astype(np.float32)
    large = max_exact + (np.log(nf / max_exact) / math.log(REL_MAX_DIST / max_exact)
                         * (REL_BUCKETS - max_exact)).astype(np.int32)
    large = np.minimum(large, REL_BUCKETS - 1)
    bucket = np.where(n < max_exact, n, large)
    thr = [int(np.argmax(bucket >= k)) for k in range(REL_BUCKETS)]
    assert all(bucket[t] == k for k, t in enumerate(thr))
    return thr


REL_THRESHOLDS = _rel_thresholds()
REL_FAR_DIST = REL_THRESHOLDS[-1]


def _layer_norm(z, g, b):
    mu = jnp.mean(z, axis=-1, keepdims=True)
    zc = z - mu
    var = jnp.mean(zc * zc, axis=-1, keepdims=True)
    return zc * lax.rsqrt(var + LN_EPS) * g + b


def _compiler_params(n_grid):
    return pltpu.CompilerParams(
        dimension_semantics=("arbitrary",) * n_grid,
        vmem_limit_bytes=VMEM_LIMIT_BYTES,
    )


def _resident(shape):
    nd = len(shape)
    return pl.BlockSpec(shape, lambda *_: (0,) * nd, pipeline_mode=pl.Buffered(1))


def _ln_inproj_kernel(x_ref, g_ref, b_ref, w_ref, h_ref, qkv_ref, mqf_ref, gl_ref, kmean_ref):
    h = _layer_norm(x_ref[...], g_ref[...], b_ref[...])
    h_ref[...] = h
    hb = h.astype(jnp.bfloat16)
    q_scale = HEAD_DIM ** -0.5 * LOG2_E
    for c0 in range(0, W_IN_COLS, PROJ_CHUNK):
        acc = jnp.dot(hb, w_ref[:, c0:c0 + PROJ_CHUNK], preferred_element_type=jnp.float32)
        if c0 < COL_GATE:
            is_q = c0 < COL_DK or COL_MQ <= c0 < COL_MK
            if COL_MQ <= c0 < COL_MK:
                mqf_ref[:, c0 - COL_MQ:c0 - COL_MQ + PROJ_CHUNK] = acc
            if COL_MK <= c0 < COL_MV:
                nblk = acc.shape[0] // MOBA_BLOCK
                km = jnp.mean(acc.reshape(nblk, MOBA_BLOCK, PROJ_CHUNK), axis=1)
                kmean_ref[0, :, c0 - COL_MK:c0 - COL_MK + PROJ_CHUNK] = km
            val = acc * q_scale if is_q else acc
            qkv_ref[:, c0:c0 + PROJ_CHUNK] = val.astype(jnp.bfloat16)
        else:
            gl_ref[:, c0 - COL_GATE:c0 - COL_GATE + PROJ_CHUNK] = acc


def _ln_inproj(x2, ln_g, ln_b, w_in_bf):
    n_rows = x2.shape[0]
    tm = ROW_TILE
    nblk = tm // MOBA_BLOCK
    grid = (n_rows // tm,)
    row = lambda cols: pl.BlockSpec((tm, cols), lambda i: (i, 0))
    return pl.pallas_call(
        _ln_inproj_kernel,
        grid=grid,
        in_specs=[row(D_MODEL), _resident((1, D_MODEL)), _resident((1, D_MODEL)),
                  _resident((D_MODEL, W_IN_COLS))],
        out_specs=[row(D_MODEL), row(N_QKV_COLS), row(MOBA_HEADS * HEAD_DIM), row(2 * D_MODEL),
                   pl.BlockSpec((1, nblk, MOBA_HEADS * HEAD_DIM), lambda i: (i, 0, 0))],
        out_shape=[
            jax.ShapeDtypeStruct((n_rows, D_MODEL), jnp.float32),
            jax.ShapeDtypeStruct((n_rows, N_QKV_COLS), jnp.bfloat16),
            jax.ShapeDtypeStruct((n_rows, MOBA_HEADS * HEAD_DIM), jnp.float32),
            jax.ShapeDtypeStruct((n_rows, 2 * D_MODEL), jnp.float32),
            jax.ShapeDtypeStruct((n_rows // tm, nblk, MOBA_HEADS * HEAD_DIM), jnp.float32),
        ],
        compiler_params=_compiler_params(1),
        name="ln_inproj",
    )(x2, ln_g, ln_b, w_in_bf)


def _moba_select_kernel(q_ref, km_ref, neg_ref, *, n_blocks):
    t = q_ref.shape[0]
    half = LANES // 2
    row0 = pl.program_id(1) * t
    shape3 = (2, half, t)
    blk = lax.broadcasted_iota(jnp.int32, shape3, 1)
    cur = (row0 + lax.broadcasted_iota(jnp.int32, shape3, 2)) // MOBA_BLOCK
    blk_f = blk.astype(jnp.float32)
    ninf = jnp.float32(-jnp.inf)
    for pair in range(MOBA_HEADS // 2):
        q = q_ref[:, pair * LANES:(pair + 1) * LANES]
        gate = lax.dot_general(km_ref[0, pair], q, (((1,), (1,)), ((), ())),
                               preferred_element_type=jnp.float32,
                               precision=lax.Precision.HIGHEST)
        g = jnp.where((blk < cur) & (blk < n_blocks), gate.reshape(shape3), ninf)
        neg = jnp.full(shape3, MASK_NEG, jnp.float32)
        for rank in range(MOBA_TOPK):
            mx = jnp.max(g, axis=1, keepdims=True)
            ix = jnp.min(jnp.where(g == mx, blk_f, float(half)), axis=1, keepdims=True)
            pick = blk_f == ix
            neg = jnp.where(pick & (rank < cur), 0.0, neg)
            g = jnp.where(pick, ninf, g)
        neg_ref[:, pair * LANES:(pair + 1) * LANES] = neg.reshape(LANES, t).T


def _moba_select(mqf, km2, batch, seq):
    t = SEL_TILE
    nq = seq // t
    n_blocks = seq // MOBA_BLOCK
    return pl.pallas_call(
        functools.partial(_moba_select_kernel, n_blocks=n_blocks),
        grid=(batch, nq),
        in_specs=[pl.BlockSpec((t, MOBA_HEADS * HEAD_DIM), lambda b, i: (b * nq + i, 0)),
                  pl.BlockSpec((1, MOBA_HEADS // 2, LANES, LANES), lambda b, i: (b, 0, 0, 0))],
        out_specs=pl.BlockSpec((t, MOBA_HEADS * HEAD_DIM), lambda b, i: (b * nq + i, 0)),
        out_shape=jax.ShapeDtypeStruct(mqf.shape, jnp.float32),
        compiler_params=_compiler_params(2),
        name="moba_select",
    )(mqf, km2)


def _fill_bias_table(tab_ref, rel_ref, head, t):
    far = rel_ref[head, REL_BUCKETS - 1]
    r = lax.broadcasted_iota(jnp.int32, (t, 2 * t), 0)
    c = lax.broadcasted_iota(jnp.int32, (t, 2 * t), 1)
    dist = r - c + t
    bias = jnp.full((t, 2 * t), (rel_ref[head, 0] - far) * LOG2_E, jnp.float32)
    for k in range(1, REL_BUCKETS):
        bias = jnp.where(dist >= REL_THRESHOLDS[k], (rel_ref[head, k] - far) * LOG2_E, bias)
    tab_ref[...] = jnp.where(dist >= 0, bias, MASK_NEG)


def _softmax_step(s, v_aug, m_ref, acc_ref):
    m_prev = m_ref[...]
    m_new = jnp.maximum(m_prev, jnp.max(s, axis=1, keepdims=True))
    alpha = jnp.exp2(m_prev - m_new)
    p = jnp.exp2(s - jnp.tile(m_new, (1, s.shape[1] // LANES)))
    pv = jnp.dot(p.astype(jnp.bfloat16), v_aug, preferred_element_type=jnp.float32)
    acc_ref[...] = jnp.tile(alpha, (1, 2)) * acc_ref[...] + pv
    m_ref[...] = m_new


def _far_tiles(n, logits_of, values_of, sa_ref, sb_ref, m_ref, acc_ref):
    def run(first, count):
        for u in range(count):
            cur, nxt = (sa_ref, sb_ref) if u % 2 == 0 else (sb_ref, sa_ref)
            nxt[...] = logits_of(jnp.minimum(first + u + 1, n - 1))
            _softmax_step(cur[...], values_of(first + u), m_ref, acc_ref)

    groups = lax.shift_right_logical(n, FAR_UNROLL.bit_length() - 1)

    def group(i, carry):
        run(i * FAR_UNROLL, FAR_UNROLL)
        return carry

    lax.fori_loop(0, groups, group, 0)
    done = groups * FAR_UNROLL
    for count in (FAR_UNROLL // 2, FAR_UNROLL // 4):
        if count >= 2:
            @pl.when(((n - done) & count) != 0)
            def _(done=done, count=count):
                run(done, count)
            done = done + ((n - done) & count)

    @pl.when((n & 1) == 1)
    def _():
        _softmax_step(sa_ref[...], values_of(n - 1), m_ref, acc_ref)


def _init_softmax(m_ref, acc_ref):
    m_ref[...] = jnp.full(m_ref.shape, MASK_NEG, jnp.float32)
    acc_ref[...] = jnp.zeros(acc_ref.shape, jnp.float32)


def _fill_values_aug(vaug_ref, v_ref):
    vaug_ref[:, 0:LANES] = v_ref[...]
    vaug_ref[:, LANES:2 * LANES] = jnp.ones(v_ref.shape, vaug_ref.dtype)


def _qk(q2, k):
    return lax.dot_general(q2, k, (((1,), (1,)), ((), ())), preferred_element_type=jnp.float32)


def _split_halves(q):
    lane = lax.broadcasted_iota(jnp.int32, q.shape, 1)
    zero = jnp.zeros_like(q)
    return jnp.concatenate([jnp.where(lane < LANES // 2, q, zero),
                            jnp.where(lane < LANES // 2, zero, q)], axis=0)


def _diff_attn_kernel(rel_ref, q_ref, k_ref, v_ref, lam_ref, subg_ref, o_ref,
                      tab_ref, vaug_ref, sa_ref, sb_ref, m_ref, acc_ref):
    t = q_ref.shape[0]
    head = pl.program_id(1)
    qi = pl.program_id(2)

    @pl.when(qi == 0)
    def _():
        _fill_bias_table(tab_ref, rel_ref, head, t)
        _fill_values_aug(vaug_ref, v_ref)

    q2 = _split_halves(q_ref[...])

    def logits_of(j):
        return _qk(q2, k_ref[pl.ds(pl.multiple_of(j * t, t), t), :])

    def values_of(j):
        return vaug_ref[pl.ds(pl.multiple_of(j * t, t), t), :]

    def near_logits(j, col0):
        bias = tab_ref[:, col0:col0 + t]
        return (logits_of(j).reshape(2, t, t) + bias[None]).reshape(2 * t, t)

    prev = jnp.maximum(qi - 1, 0)
    _init_softmax(m_ref, acc_ref)
    sa_ref[...] = near_logits(qi, t)
    sb_ref[...] = near_logits(prev, 0)
    _softmax_step(sa_ref[...], values_of(qi), m_ref, acc_ref)

    @pl.when(qi >= 1)
    def _():
        sa_ref[...] = logits_of(0)
        _softmax_step(sb_ref[...], values_of(prev), m_ref, acc_ref)

    _far_tiles(jnp.maximum(qi - 1, 0), logits_of, values_of, sa_ref, sb_ref, m_ref, acc_ref)

    lam_p = lam_ref[...]
    lam = (jnp.exp(jnp.sum(lam_p[0:1] * lam_p[1:2], axis=1, keepdims=True))
           - jnp.exp(jnp.sum(lam_p[2:3] * lam_p[3:4], axis=1, keepdims=True)) + LAM_INIT)
    o = acc_ref[:, 0:LANES] / acc_ref[:, LANES:2 * LANES]
    o = o[:t] - lam * o[t:]
    o = o * lax.rsqrt(jnp.mean(o * o, axis=1, keepdims=True) + LN_EPS) * subg_ref[...]
    o_ref[...] = (o * (1.0 - LAM_INIT)).astype(o_ref.dtype)


def _diff_attn(rel_t, qkv, lam_p, sub_g, batch, seq):
    t = DIFF_TILE
    nq = seq // t
    grid_spec = pltpu.PrefetchScalarGridSpec(
        num_scalar_prefetch=1,
        grid=(batch, DIFF_HEADS, nq),
        in_specs=[
            pl.BlockSpec((t, LANES), lambda b, h, i, rel: (b * nq + i, COL_DQ // LANES + h)),
            pl.BlockSpec((seq, LANES), lambda b, h, i, rel: (b, COL_DK // LANES + h)),
            pl.BlockSpec((seq, LANES), lambda b, h, i, rel: (b, COL_DV // LANES + h)),
            pl.BlockSpec((4, HEAD_DIM), lambda b, h, i, rel: (0, 0)),
            pl.BlockSpec((1, LANES), lambda b, h, i, rel: (0, 0)),
        ],
        out_specs=pl.BlockSpec((t, LANES), lambda b, h, i, rel: (b * nq + i, h)),
        scratch_shapes=[
            pltpu.VMEM((t, 2 * t), jnp.float32),
            pltpu.VMEM((seq, 2 * LANES), jnp.bfloat16),
            pltpu.VMEM((2 * t, t), jnp.float32),
            pltpu.VMEM((2 * t, t), jnp.float32),
            pltpu.VMEM((2 * t, LANES), jnp.float32),
            pltpu.VMEM((2 * t, 2 * LANES), jnp.float32),
        ],
    )
    return pl.pallas_call(
        _diff_attn_kernel,
        grid_spec=grid_spec,
        out_shape=jax.ShapeDtypeStruct((batch * seq, DIFF_HEADS * LANES), jnp.bfloat16),
        compiler_params=_compiler_params(3),
        name="diff_attn",
    )(rel_t, qkv, qkv, qkv, lam_p, sub_g)


def _moba_attn_kernel(rel_ref, q_ref, k_ref, v_ref, neg_ref, o_ref,
                      tab_ref, kaug_ref, vaug_ref, lhs_ref, sa_ref, sb_ref, m_ref, acc_ref):
    t = q_ref.shape[0]
    seq = k_ref.shape[0]
    half = LANES // 2
    blocks_per_tile = t // MOBA_BLOCK
    pair = pl.program_id(1)
    qi = pl.program_id(2)

    @pl.when(qi == 0)
    def _():
        _fill_bias_table(tab_ref.at[0], rel_ref, DIFF_HEADS + 2 * pair, t)
        _fill_bias_table(tab_ref.at[1], rel_ref, DIFF_HEADS + 2 * pair + 1, t)
        _fill_values_aug(vaug_ref, v_ref)
        kaug_ref[:, 0:LANES] = k_ref[...]
        key_blk = lax.broadcasted_iota(jnp.int32, (seq, LANES), 0) // MOBA_BLOCK
        lane_k = lax.broadcasted_iota(jnp.int32, (seq, LANES), 1)
        kaug_ref[:, LANES:2 * LANES] = jnp.where(key_blk == lane_k, 1.0, 0.0).astype(kaug_ref.dtype)

    q2 = _split_halves(q_ref[...])
    neg = neg_ref[...]
    neg2 = jnp.concatenate([neg, pltpu.roll(neg, half, axis=1)], axis=0)
    lhs_ref[:, 0:LANES] = q2
    lhs_ref[:, LANES:2 * LANES] = neg2.astype(lhs_ref.dtype)

    def keys_of(j):
        return kaug_ref[pl.ds(pl.multiple_of(j * t, t), t), :]

    def logits_of(j):
        return _qk(lhs_ref[...], keys_of(j))

    def values_of(j):
        return vaug_ref[pl.ds(pl.multiple_of(j * t, t), t), :]

    prev = jnp.maximum(qi - 1, 0)
    _init_softmax(m_ref, acc_ref)
    s = _qk(q2, keys_of(qi)[:, 0:LANES]).reshape(2, t, t) + tab_ref[:, :, t:2 * t]
    row_blk = lax.broadcasted_iota(jnp.int32, (2, t, t), 1) // MOBA_BLOCK
    col_blk = lax.broadcasted_iota(jnp.int32, (2, t, t), 2) // MOBA_BLOCK
    lane = lax.broadcasted_iota(jnp.int32, (2 * t, LANES), 1)
    for cb in range(blocks_per_tile - 1):
        chosen = jnp.sum(jnp.where(lane == qi * blocks_per_tile + cb, neg2, 0.0),
                         axis=1, keepdims=True).reshape(2, t, 1)
        s = s + jnp.where((col_blk == cb) & (row_blk > cb), chosen, 0.0)
    sa_ref[...] = s.reshape(2 * t, t)
    sb_ref[...] = (logits_of(prev).reshape(2, t, t) + tab_ref[:, :, 0:t]).reshape(2 * t, t)
    _softmax_step(sa_ref[...], values_of(qi), m_ref, acc_ref)

    @pl.when(qi >= 1)
    def _():
        sa_ref[...] = logits_of(0)
        _softmax_step(sb_ref[...], values_of(prev), m_ref, acc_ref)

    _far_tiles(jnp.maximum(qi - 1, 0), logits_of, values_of, sa_ref, sb_ref, m_ref, acc_ref)

    o = acc_ref[:, 0:LANES] / acc_ref[:, LANES:2 * LANES]
    lane_o = lax.broadcasted_iota(jnp.int32, (t, LANES), 1)
    o_ref[...] = jnp.where(lane_o < half, o[:t], o[t:]).astype(o_ref.dtype)


def _moba_attn(rel_t, qkv, neg, batch, seq):
    t = MOBA_TILE
    nq = seq // t
    n_pairs = MOBA_HEADS // 2
    grid_spec = pltpu.PrefetchScalarGridSpec(
        num_scalar_prefetch=1,
        grid=(batch, n_pairs, nq),
        in_specs=[
            pl.BlockSpec((t, LANES), lambda b, p, i, rel: (b * nq + i, COL_MQ // LANES + p)),
            pl.BlockSpec((seq, LANES), lambda b, p, i, rel: (b, COL_MK // LANES + p)),
            pl.BlockSpec((seq, LANES), lambda b, p, i, rel: (b, COL_MV // LANES + p)),
            pl.BlockSpec((t, LANES), lambda b, p, i, rel: (b * nq + i, p)),
        ],
        out_specs=pl.BlockSpec((t, LANES), lambda b, p, i, rel: (b * nq + i, p)),
        scratch_shapes=[
            pltpu.VMEM((2, t, 2 * t), jnp.float32),
            pltpu.VMEM((seq, 2 * LANES), jnp.bfloat16),
            pltpu.VMEM((seq, 2 * LANES), jnp.bfloat16),
            pltpu.VMEM((2 * t, 2 * LANES), jnp.bfloat16),
            pltpu.VMEM((2 * t, t), jnp.float32),
            pltpu.VMEM((2 * t, t), jnp.float32),
            pltpu.VMEM((2 * t, LANES), jnp.float32),
            pltpu.VMEM((2 * t, 2 * LANES), jnp.float32),
        ],
    )
    return pl.pallas_call(
        _moba_attn_kernel,
        grid_spec=grid_spec,
        out_shape=jax.ShapeDtypeStruct((batch * seq, MOBA_HEADS * HEAD_DIM), jnp.bfloat16),
        compiler_params=_compiler_params(3),
        name="moba_attn",
    )(rel_t, qkv, qkv, qkv, neg)


def _merge_kernel(ad_ref, am_ref, gl_ref, bg_ref, h_ref, wbd_ref, wbm_ref, wo_ref,
                  g_ref, b_ref, o_ref):
    y_d = jnp.dot(ad_ref[...], wbd_ref[...], preferred_element_type=jnp.float32)
    y_m = jnp.dot(am_ref[...], wbm_ref[...], preferred_element_type=jnp.float32)
    gates = jax.nn.sigmoid(gl_ref[...] + bg_ref[...])
    mixed = gates[:, :D_MODEL] * y_d + gates[:, D_MODEL:] * y_m
    out = jnp.dot(mixed.astype(jnp.bfloat16), wo_ref[...], preferred_element_type=jnp.float32)
    o_ref[...] = _layer_norm(DEEPNORM_ALPHA * h_ref[...] + out, g_ref[...], b_ref[...])


def _merge(attn_d, attn_m, gl, b_gate, h, wbd, wbm, wout, ln_g, ln_b):
    n_rows = h.shape[0]
    tm = ROW_TILE
    row = lambda cols: pl.BlockSpec((tm, cols), lambda i: (i, 0))
    return pl.pallas_call(
        _merge_kernel,
        grid=(n_rows // tm,),
        in_specs=[row(D_MODEL), row(MOBA_HEADS * HEAD_DIM), row(2 * D_MODEL),
                  _resident((1, 2 * D_MODEL)), row(D_MODEL),
                  _resident(wbd.shape), _resident(wbm.shape), _resident(wout.shape),
                  _resident((1, D_MODEL)), _resident((1, D_MODEL))],
        out_specs=row(D_MODEL),
        out_shape=jax.ShapeDtypeStruct((n_rows, D_MODEL), jnp.float32),
        compiler_params=_compiler_params(1),
        name="merge",
    )(attn_d, attn_m, gl, b_gate, h, wbd, wbm, wout, ln_g, ln_b)


def _xattn_kv_kernel(mem_ref, wk_ref, wv_ref, k_ref, v_ref):
    mb = mem_ref[...].astype(jnp.bfloat16)
    k_ref[...] = jnp.dot(mb, wk_ref[...], preferred_element_type=jnp.float32).astype(k_ref.dtype)
    v_ref[...] = jnp.dot(mb, wv_ref[...], preferred_element_type=jnp.float32).astype(v_ref.dtype)


def _xattn_kv(mem2, wk, wv, batch, n_mem):
    blk = pl.BlockSpec((n_mem, D_MODEL), lambda b: (b, 0))
    return pl.pallas_call(
        _xattn_kv_kernel,
        grid=(batch,),
        in_specs=[blk, _resident(wk.shape), _resident(wv.shape)],
        out_specs=[blk, blk],
        out_shape=[jax.ShapeDtypeStruct(mem2.shape, jnp.bfloat16)] * 2,
        compiler_params=_compiler_params(1),
        name="xattn_kv",
    )(mem2, wk, wv)


def _xattn_kernel(h_ref, wq_ref, k_ref, v_ref, wo_ref, g_ref, b_ref, o_ref):
    h = h_ref[...]
    q = jnp.dot(h.astype(jnp.bfloat16), wq_ref[...], preferred_element_type=jnp.float32)
    q = (q * XATTN_HEAD_DIM ** -0.5).astype(jnp.bfloat16)
    outs = []
    for hd in range(XATTN_HEADS):
        cols = slice(hd * XATTN_HEAD_DIM, (hd + 1) * XATTN_HEAD_DIM)
        s = _qk(q[:, cols], k_ref[:, cols])
        p = jnp.exp(s - jnp.max(s, axis=1, keepdims=True))
        p = p / jnp.sum(p, axis=1, keepdims=True)
        outs.append(jnp.dot(p.astype(jnp.bfloat16), v_ref[:, cols],
                            preferred_element_type=jnp.float32))
    o = jnp.concatenate(outs, axis=1).astype(jnp.bfloat16)
    xa = jnp.dot(o, wo_ref[...], preferred_element_type=jnp.float32)
    o_ref[...] = _layer_norm(DEEPNORM_ALPHA * h + xa, g_ref[...], b_ref[...])


def _xattn(h1, wq, kx, vx, wo, ln_g, ln_b, batch, seq, n_mem):
    tm = ROW_TILE
    nq = seq // tm
    row = pl.BlockSpec((tm, D_MODEL), lambda b, i: (b * nq + i, 0))
    kv = pl.BlockSpec((n_mem, D_MODEL), lambda b, i: (b, 0))
    return pl.pallas_call(
        _xattn_kernel,
        grid=(batch, nq),
        in_specs=[row, _resident(wq.shape), kv, kv, _resident(wo.shape),
                  _resident((1, D_MODEL)), _resident((1, D_MODEL))],
        out_specs=row,
        out_shape=jax.ShapeDtypeStruct(h1.shape, jnp.float32),
        compiler_params=_compiler_params(2),
        name="xattn",
    )(h1, wq, kx, vx, wo, ln_g, ln_b)


def _mlp_kernel(h_ref, w1_ref, w2_ref, g_ref, b_ref, o_ref, hb_ref, acc_ref):
    j = pl.program_id(1)

    @pl.when(j == 0)
    def _():
        hb_ref[...] = h_ref[...].astype(jnp.bfloat16)
        acc_ref[...] = jnp.zeros(acc_ref.shape, jnp.float32)

    u = jnp.dot(hb_ref[...], w1_ref[...], preferred_element_type=jnp.float32)
    u = jnp.square(jnp.maximum(u, 0.0)).astype(jnp.bfloat16)
    acc_ref[...] += jnp.dot(u, w2_ref[...], preferred_element_type=jnp.float32)

    @pl.when(j == pl.num_programs(1) - 1)
    def _():
        o_ref[...] = _layer_norm(DEEPNORM_ALPHA * h_ref[...] + acc_ref[...],
                                 g_ref[...], b_ref[...])


def _mlp(h2, w1, w2, ln_g, ln_b):
    n_rows = h2.shape[0]
    tm = ROW_TILE
    row = pl.BlockSpec((tm, D_MODEL), lambda i, j: (i, 0))
    return pl.pallas_call(
        _mlp_kernel,
        grid=(n_rows // tm, D_FF // FF_TILE),
        in_specs=[row,
                  pl.BlockSpec((D_MODEL, FF_TILE), lambda i, j: (0, j)),
                  pl.BlockSpec((FF_TILE, D_MODEL), lambda i, j: (j, 0)),
                  _resident((1, D_MODEL)), _resident((1, D_MODEL))],
        out_specs=row,
        out_shape=jax.ShapeDtypeStruct(h2.shape, jnp.float32),
        scratch_shapes=[pltpu.VMEM((tm, D_MODEL), jnp.bfloat16),
                        pltpu.VMEM((tm, D_MODEL), jnp.float32)],
        compiler_params=_compiler_params(2),
        name="mlp",
    )(h2, w1, w2, ln_g, ln_b)


def kernel(x, mem, ln_in_g, ln_in_b, rel_table, w_in, b_gate, lam_q1, lam_k1, lam_q2, lam_k2,
           diff_sub_g, w_br_diff, w_br_moba, w_out, ln1_g, ln1_b, wq_x, wk_x, wv_x, wo_x,
           ln2_g, ln2_b, w_ff1, w_ff2, ln3_g, ln3_b):
    batch, seq, d = x.shape
    n_mem = mem.shape[1]
    assert d == D_MODEL and w_in.shape == (DEPTH, D_MODEL, W_IN_COLS)
    assert seq % ROW_TILE == 0 and seq % DIFF_TILE == 0 and seq % SEL_TILE == 0
    assert seq % MOBA_TILE == 0 and MOBA_TILE % MOBA_BLOCK == 0
    assert seq // MOBA_BLOCK <= LANES // 2 and MOBA_TILE >= REL_FAR_DIST <= DIFF_TILE
    bf = lambda w: w.astype(jnp.bfloat16)
    vec = lambda v: v.reshape(1, -1).astype(jnp.float32)
    l = 0

    x2 = x.reshape(batch * seq, d)
    h, qkv, mqf, gl, kmean = _ln_inproj(x2, vec(ln_in_g), vec(ln_in_b), bf(w_in[l]))

    n_blocks = seq // MOBA_BLOCK
    km = kmean.reshape(batch, n_blocks, MOBA_HEADS // 2, 2, HEAD_DIM)
    km = jnp.pad(km, ((0, 0), (0, LANES // 2 - n_blocks), (0, 0), (0, 0), (0, 0)))
    eye2 = jnp.eye(2, dtype=jnp.float32)
    km2 = jnp.einsum('bnpad,ac->bpcnad', km, eye2).reshape(batch, MOBA_HEADS // 2, LANES, LANES)
    neg = _moba_select(mqf, km2, batch, seq)

    rel_t = rel_table.T.astype(jnp.float32)
    lam_p = jnp.concatenate([lam_q1[l:l + 1], lam_k1[l:l + 1], lam_q2[l:l + 1], lam_k2[l:l + 1]],
                            axis=0).astype(jnp.float32)
    attn_d = _diff_attn(rel_t, qkv, lam_p, vec(diff_sub_g[l]), batch, seq)
    attn_m = _moba_attn(rel_t, qkv, neg, batch, seq)

    h1 = _merge(attn_d, attn_m, gl, vec(b_gate[l]), h, bf(w_br_diff[l]), bf(w_br_moba[l]),
                bf(w_out[l]), vec(ln1_g[l]), vec(ln1_b[l]))

    kx, vx = _xattn_kv(mem.reshape(batch * n_mem, d), bf(wk_x[l]), bf(wv_x[l]), batch, n_mem)
    h2 = _xattn(h1, bf(wq_x[l]), kx, vx, bf(wo_x[l]), vec(ln2_g[l]), vec(ln2_b[l]),
                batch, seq, n_mem)

    h3 = _mlp(h2, bf(w_ff1[l]), bf(w_ff2[l]), vec(ln3_g[l]), vec(ln3_b[l]))
    return h3.reshape(batch, seq, d)
```

```python
import functools
import math

import numpy as np
import jax
import jax.numpy as jnp
from jax import lax
from jax.experimental import pallas as pl
from jax.experimental.pallas import tpu as pltpu

D_MODEL = 1024
HEAD_DIM = 64
DIFF_HEADS = 8
MOBA_HEADS = 8
MOBA_BLOCK = 256
MOBA_TOPK = 3
XATTN_HEADS = 4
XATTN_HEAD_DIM = D_MODEL // XATTN_HEADS
D_FF = 4 * D_MODEL
REL_BUCKETS = 32
REL_MAX_DIST = 128
LN_EPS = 1e-5
DEPTH = 1
DEEPNORM_ALPHA = (2.0 * DEPTH) ** 0.25
LAM_INIT = 0.8 - 0.6 * math.exp(-0.3 * 0)
LOG2_E = math.log2(math.e)

COL_DQ = 0
COL_DK = 1024
COL_DV = 2048
COL_MQ = 3072
COL_MK = 3584
COL_MV = 4096
COL_GATE = 4608
N_QKV_COLS = 4608
W_IN_COLS = 6656

LANES = 128
VMEM_LIMIT_BYTES = 56 * 1024 * 1024

ROW_TILE = 512
DIFF_TILE = 512
MOBA_TILE = 512
SEL_TILE = 512
FF_TILE = 1024
PROJ_CHUNK = 512
FAR_UNROLL = 8
ROW_CHAINS = 2
MLP_ROW_TILE = 1024

MASK_NEG = -1e30


def _rel_thresholds():
    max_exact = REL_BUCKETS // 2
    n = np.arange(0, 4 * REL_MAX_DIST)
    nf = np.maximum(n, 1).astype(np.float32)
    large = max_exact + (np.log(nf / max_exact) / math.log(REL_MAX_DIST / max_exact)
                         * (REL_BUCKETS - max_exact)).astype(np.int32)
    large = np.minimum(large, REL_BUCKETS - 1)
    bucket = np.where(n < max_exact, n, large)
    thr = [int(np.argmax(bucket >= k)) for k in range(REL_BUCKETS)]
    assert all(bucket[t] == k for k, t in enumerate(thr))
    return thr


REL_THRESHOLDS = _rel_thresholds()
REL_FAR_DIST = REL_THRESHOLDS[-1]


def _layer_norm(z, g, b):
    mu = jnp.mean(z, axis=-1, keepdims=True)
    zc = z - mu
    var = jnp.mean(zc * zc, axis=-1, keepdims=True)
    return zc * lax.rsqrt(var + LN_EPS) * g + b


def _compiler_params(n_grid):
    return pltpu.CompilerParams(
        dimension_semantics=("arbitrary",) * n_grid,
        vmem_limit_bytes=VMEM_LIMIT_BYTES,
    )


def _row_chains(n_rows):
    step = n_rows // ROW_CHAINS
    return [pl.ds(r0, step) for r0 in range(0, n_rows, step)]


def _resident(shape):
    nd = len(shape)
    return pl.BlockSpec(shape, lambda *_: (0,) * nd, pipeline_mode=pl.Buffered(1))


def _ln_inproj_kernel(x_ref, g_ref, b_ref, w_ref, h_ref, qkv_ref, mqf_ref, gl_ref, kmean_ref):
    h = _layer_norm(x_ref[...], g_ref[...], b_ref[...])
    h_ref[...] = h
    hb = h.astype(jnp.bfloat16)
    q_scale = HEAD_DIM ** -0.5 * LOG2_E
    for c0 in range(0, W_IN_COLS, PROJ_CHUNK):
        acc = jnp.dot(hb, w_ref[:, c0:c0 + PROJ_CHUNK], preferred_element_type=jnp.float32)
        if c0 < COL_GATE:
            is_q = c0 < COL_DK or COL_MQ <= c0 < COL_MK
            if COL_MQ <= c0 < COL_MK:
                mqf_ref[:, c0 - COL_MQ:c0 - COL_MQ + PROJ_CHUNK] = acc
            if COL_MK <= c0 < COL_MV:
                nblk = acc.shape[0] // MOBA_BLOCK
                km = jnp.mean(acc.reshape(nblk, MOBA_BLOCK, PROJ_CHUNK), axis=1)
                kmean_ref[0, :, c0 - COL_MK:c0 - COL_MK + PROJ_CHUNK] = km
            val = acc * q_scale if is_q else acc
            qkv_ref[:, c0:c0 + PROJ_CHUNK] = val.astype(jnp.bfloat16)
        else:
            gl_ref[:, c0 - COL_GATE:c0 - COL_GATE + PROJ_CHUNK] = acc


def _ln_inproj(x2, ln_g, ln_b, w_in_bf):
    n_rows = x2.shape[0]
    tm = ROW_TILE
    nblk = tm // MOBA_BLOCK
    grid = (n_rows // tm,)
    row = lambda cols: pl.BlockSpec((tm, cols), lambda i: (i, 0))
    return pl.pallas_call(
        _ln_inproj_kernel,
        grid=grid,
        in_specs=[row(D_MODEL), _resident((1, D_MODEL)), _resident((1, D_MODEL)),
                  _resident((D_MODEL, W_IN_COLS))],
        out_specs=[row(D_MODEL), row(N_QKV_COLS), row(MOBA_HEADS * HEAD_DIM), row(2 * D_MODEL),
                   pl.BlockSpec((1, nblk, MOBA_HEADS * HEAD_DIM), lambda i: (i, 0, 0))],
        out_shape=[
            jax.ShapeDtypeStruct((n_rows, D_MODEL), jnp.float32),
            jax.ShapeDtypeStruct((n_rows, N_QKV_COLS), jnp.bfloat16),
            jax.ShapeDtypeStruct((n_rows, MOBA_HEADS * HEAD_DIM), jnp.float32),
            jax.ShapeDtypeStruct((n_rows, 2 * D_MODEL), jnp.float32),
            jax.ShapeDtypeStruct((n_rows // tm, nblk, MOBA_HEADS * HEAD_DIM), jnp.float32),
        ],
        compiler_params=_compiler_params(1),
        name="ln_inproj",
    )(x2, ln_g, ln_b, w_in_bf)


def _moba_select_kernel(q_ref, km_ref, neg_ref, *, n_blocks):
    t = q_ref.shape[0]
    half = LANES // 2
    row0 = pl.program_id(1) * t
    shape3 = (2, half, t)
    blk = lax.broadcasted_iota(jnp.int32, shape3, 1)
    cur = (row0 + lax.broadcasted_iota(jnp.int32, shape3, 2)) // MOBA_BLOCK
    blk_f = blk.astype(jnp.float32)
    ninf = jnp.float32(-jnp.inf)
    for pair in range(MOBA_HEADS // 2):
        q = q_ref[:, pair * LANES:(pair + 1) * LANES]
        gate = lax.dot_general(km_ref[0, pair], q, (((1,), (1,)), ((), ())),
                               preferred_element_type=jnp.float32,
                               precision=lax.Precision.HIGHEST)
        g = jnp.where((blk < cur) & (blk < n_blocks), gate.reshape(shape3), ninf)
        neg = jnp.full(shape3, MASK_NEG, jnp.float32)
        for rank in range(MOBA_TOPK):
            mx = jnp.max(g, axis=1, keepdims=True)
            ix = jnp.min(jnp.where(g == mx, blk_f, float(half)), axis=1, keepdims=True)
            pick = blk_f == ix
            neg = jnp.where(pick & (rank < cur), 0.0, neg)
            g = jnp.where(pick, ninf, g)
        neg_ref[:, pair * LANES:(pair + 1) * LANES] = neg.reshape(LANES, t).T


def _moba_select(mqf, km2, batch, seq):
    t = SEL_TILE
    nq = seq // t
    n_blocks = seq // MOBA_BLOCK
    return pl.pallas_call(
        functools.partial(_moba_select_kernel, n_blocks=n_blocks),
        grid=(batch, nq),
        in_specs=[pl.BlockSpec((t, MOBA_HEADS * HEAD_DIM), lambda b, i: (b * nq + i, 0)),
                  pl.BlockSpec((1, MOBA_HEADS // 2, LANES, LANES), lambda b, i: (b, 0, 0, 0))],
        out_specs=pl.BlockSpec((t, MOBA_HEADS * HEAD_DIM), lambda b, i: (b * nq + i, 0)),
        out_shape=jax.ShapeDtypeStruct(mqf.shape, jnp.float32),
        compiler_params=_compiler_params(2),
        name="moba_select",
    )(mqf, km2)


def _fill_bias_table(tab_ref, rel_ref, head, t):
    far = rel_ref[head, REL_BUCKETS - 1]
    r = lax.broadcasted_iota(jnp.int32, (t, 2 * t), 0)
    c = lax.broadcasted_iota(jnp.int32, (t, 2 * t), 1)
    dist = r - c + t
    bias = jnp.full((t, 2 * t), (rel_ref[head, 0] - far) * LOG2_E, jnp.float32)
    for k in range(1, REL_BUCKETS):
        bias = jnp.where(dist >= REL_THRESHOLDS[k], (rel_ref[head, k] - far) * LOG2_E, bias)
    tab_ref[...] = jnp.where(dist >= 0, bias, MASK_NEG)


def _softmax_step(s, v_aug, m_ref, acc_ref):
    m_prev = m_ref[...]
    m_new = jnp.maximum(m_prev, jnp.max(s, axis=1, keepdims=True))
    alpha = jnp.exp2(m_prev - m_new)
    p = jnp.exp2(s - jnp.tile(m_new, (1, s.shape[1] // LANES)))
    pv = jnp.dot(p.astype(jnp.bfloat16), v_aug, preferred_element_type=jnp.float32)
    acc_ref[...] = jnp.tile(alpha, (1, 2)) * acc_ref[...] + pv
    m_ref[...] = m_new


def _far_tiles(n, logits_of, values_of, sa_ref, sb_ref, m_ref, acc_ref):
    def run(first, count):
        for u in range(count):
            cur, nxt = (sa_ref, sb_ref) if u % 2 == 0 else (sb_ref, sa_ref)
            nxt[...] = logits_of(jnp.minimum(first + u + 1, n - 1))
            _softmax_step(cur[...], values_of(first + u), m_ref, acc_ref)

    groups = lax.shift_right_logical(n, FAR_UNROLL.bit_length() - 1)

    def group(i, carry):
        run(i * FAR_UNROLL, FAR_UNROLL)
        return carry

    lax.fori_loop(0, groups, group, 0)
    done = groups * FAR_UNROLL
    for count in (FAR_UNROLL // 2, FAR_UNROLL // 4):
        if count >= 2:
            @pl.when(((n - done) & count) != 0)
            def _(done=done, count=count):
                run(done, count)
            done = done + ((n - done) & count)

    @pl.when((n & 1) == 1)
    def _():
        _softmax_step(sa_ref[...], values_of(n - 1), m_ref, acc_ref)


def _init_softmax(m_ref, acc_ref):
    m_ref[...] = jnp.full(m_ref.shape, MASK_NEG, jnp.float32)
    acc_ref[...] = jnp.zeros(acc_ref.shape, jnp.float32)


def _fill_values_aug(vaug_ref, v_ref):
    vaug_ref[:, 0:LANES] = v_ref[...]
    vaug_ref[:, LANES:2 * LANES] = jnp.ones(v_ref.shape, vaug_ref.dtype)


def _qk(q2, k):
    return lax.dot_general(q2, k, (((1,), (1,)), ((), ())), preferred_element_type=jnp.float32)


def _split_halves(q):
    lane = lax.broadcasted_iota(jnp.int32, q.shape, 1)
    zero = jnp.zeros_like(q)
    return jnp.concatenate([jnp.where(lane < LANES // 2, q, zero),
                            jnp.where(lane < LANES // 2, zero, q)], axis=0)


def _diff_attn_kernel(rel_ref, q_ref, k_ref, v_ref, lam_ref, subg_ref, o_ref,
                      tab_ref, vaug_ref, sa_ref, sb_ref, m_ref, acc_ref):
    t = q_ref.shape[0]
    head = pl.program_id(1)
    qi = pl.program_id(2)

    @pl.when(qi == 0)
    def _():
        _fill_bias_table(tab_ref, rel_ref, head, t)
        _fill_values_aug(vaug_ref, v_ref)

    q2 = _split_halves(q_ref[...])

    def logits_of(j):
        return _qk(q2, k_ref[pl.ds(pl.multiple_of(j * t, t), t), :])

    def values_of(j):
        return vaug_ref[pl.ds(pl.multiple_of(j * t, t), t), :]

    def near_logits(j, col0):
        bias = tab_ref[:, col0:col0 + t]
        return (logits_of(j).reshape(2, t, t) + bias[None]).reshape(2 * t, t)

    prev = jnp.maximum(qi - 1, 0)
    _init_softmax(m_ref, acc_ref)
    k_own = k_ref[pl.ds(pl.multiple_of(qi * t, t), t), :]
    sa_ref[0:t, :] = _qk(q2[0:t], k_own) + tab_ref[:, t:2 * t]
    sa_ref[t:2 * t, :] = _qk(q2[t:2 * t], k_own) + tab_ref[:, t:2 * t]
    sb_ref[...] = near_logits(prev, 0)
    _softmax_step(sa_ref[...], values_of(qi), m_ref, acc_ref)

    @pl.when(qi >= 1)
    def _():
        sa_ref[...] = logits_of(0)
        _softmax_step(sb_ref[...], values_of(prev), m_ref, acc_ref)

    _far_tiles(jnp.maximum(qi - 1, 0), logits_of, values_of, sa_ref, sb_ref, m_ref, acc_ref)

    lam_p = lam_ref[...]
    lam = (jnp.exp(jnp.sum(lam_p[0:1] * lam_p[1:2], axis=1, keepdims=True))
           - jnp.exp(jnp.sum(lam_p[2:3] * lam_p[3:4], axis=1, keepdims=True)) + LAM_INIT)
    o = acc_ref[:, 0:LANES] / acc_ref[:, LANES:2 * LANES]
    o = o[:t] - lam * o[t:]
    o = o * lax.rsqrt(jnp.mean(o * o, axis=1, keepdims=True) + LN_EPS) * subg_ref[...]
    o_ref[...] = (o * (1.0 - LAM_INIT)).astype(o_ref.dtype)


def _diff_attn(rel_t, qkv, lam_p, sub_g, batch, seq):
    t = DIFF_TILE
    nq = seq // t
    grid_spec = pltpu.PrefetchScalarGridSpec(
        num_scalar_prefetch=1,
        grid=(batch, DIFF_HEADS, nq),
        in_specs=[
            pl.BlockSpec((t, LANES), lambda b, h, i, rel: (b * nq + i, COL_DQ // LANES + h)),
            pl.BlockSpec((seq, LANES), lambda b, h, i, rel: (b, COL_DK // LANES + h)),
            pl.BlockSpec((seq, LANES), lambda b, h, i, rel: (b, COL_DV // LANES + h)),
            pl.BlockSpec((4, HEAD_DIM), lambda b, h, i, rel: (0, 0)),
            pl.BlockSpec((1, LANES), lambda b, h, i, rel: (0, 0)),
        ],
        out_specs=pl.BlockSpec((t, LANES), lambda b, h, i, rel: (b * nq + i, h)),
        scratch_shapes=[
            pltpu.VMEM((t, 2 * t), jnp.float32),
            pltpu.VMEM((seq, 2 * LANES), jnp.bfloat16),
            pltpu.VMEM((2 * t, t), jnp.float32),
            pltpu.VMEM((2 * t, t), jnp.float32),
            pltpu.VMEM((2 * t, LANES), jnp.float32),
            pltpu.VMEM((2 * t, 2 * LANES), jnp.float32),
        ],
    )
    return pl.pallas_call(
        _diff_attn_kernel,
        grid_spec=grid_spec,
        out_shape=jax.ShapeDtypeStruct((batch * seq, DIFF_HEADS * LANES), jnp.bfloat16),
        compiler_params=_compiler_params(3),
        name="diff_attn",
    )(rel_t, qkv, qkv, qkv, lam_p, sub_g)


def _moba_attn_kernel(rel_ref, q_ref, k_ref, v_ref, neg_ref, o_ref,
                      tab_ref, kaug_ref, vaug_ref, lhs_ref, sa_ref, sb_ref, m_ref, acc_ref):
    t = q_ref.shape[0]
    seq = k_ref.shape[0]
    half = LANES // 2
    blocks_per_tile = t // MOBA_BLOCK
    pair = pl.program_id(1)
    qi = pl.program_id(2)

    @pl.when(qi == 0)
    def _():
        _fill_bias_table(tab_ref.at[0], rel_ref, DIFF_HEADS + 2 * pair, t)
        _fill_bias_table(tab_ref.at[1], rel_ref, DIFF_HEADS + 2 * pair + 1, t)
        _fill_values_aug(vaug_ref, v_ref)
        kaug_ref[:, 0:LANES] = k_ref[...]
        key_blk = lax.broadcasted_iota(jnp.int32, (seq, LANES), 0) // MOBA_BLOCK
        lane_k = lax.broadcasted_iota(jnp.int32, (seq, LANES), 1)
        kaug_ref[:, LANES:2 * LANES] = jnp.where(key_blk == lane_k, 1.0, 0.0).astype(kaug_ref.dtype)

    q2 = _split_halves(q_ref[...])
    neg = neg_ref[...]
    neg2 = jnp.concatenate([neg, pltpu.roll(neg, half, axis=1)], axis=0)
    lhs_ref[:, 0:LANES] = q2
    lhs_ref[:, LANES:2 * LANES] = neg2.astype(lhs_ref.dtype)

    def keys_of(j):
        return kaug_ref[pl.ds(pl.multiple_of(j * t, t), t), :]

    def logits_of(j):
        return _qk(lhs_ref[...], keys_of(j))

    def values_of(j):
        return vaug_ref[pl.ds(pl.multiple_of(j * t, t), t), :]

    prev = jnp.maximum(qi - 1, 0)
    _init_softmax(m_ref, acc_ref)
    k_own = keys_of(qi)[:, 0:LANES]
    s = jnp.stack([_qk(q2[0:t], k_own), _qk(q2[t:2 * t], k_own)]) + tab_ref[:, :, t:2 * t]
    row_blk = lax.broadcasted_iota(jnp.int32, (2, t, t), 1) // MOBA_BLOCK
    col_blk = lax.broadcasted_iota(jnp.int32, (2, t, t), 2) // MOBA_BLOCK
    lane = lax.broadcasted_iota(jnp.int32, (2 * t, LANES), 1)
    for cb in range(blocks_per_tile - 1):
        chosen = jnp.sum(jnp.where(lane == qi * blocks_per_tile + cb, neg2, 0.0),
                         axis=1, keepdims=True).reshape(2, t, 1)
        s = s + jnp.where((col_blk == cb) & (row_blk > cb), chosen, 0.0)
    sa_ref[...] = s.reshape(2 * t, t)
    sb_ref[...] = (logits_of(prev).reshape(2, t, t) + tab_ref[:, :, 0:t]).reshape(2 * t, t)
    _softmax_step(sa_ref[...], values_of(qi), m_ref, acc_ref)

    @pl.when(qi >= 1)
    def _():
        sa_ref[...] = logits_of(0)
        _softmax_step(sb_ref[...], values_of(prev), m_ref, acc_ref)

    _far_tiles(jnp.maximum(qi - 1, 0), logits_of, values_of, sa_ref, sb_ref, m_ref, acc_ref)

    o = acc_ref[:, 0:LANES] / acc_ref[:, LANES:2 * LANES]
    lane_o = lax.broadcasted_iota(jnp.int32, (t, LANES), 1)
    o_ref[...] = jnp.where(lane_o < half, o[:t], o[t:]).astype(o_ref.dtype)


def _moba_attn(rel_t, qkv, neg, batch, seq):
    t = MOBA_TILE
    nq = seq // t
    n_pairs = MOBA_HEADS // 2
    grid_spec = pltpu.PrefetchScalarGridSpec(
        num_scalar_prefetch=1,
        grid=(batch, n_pairs, nq),
        in_specs=[
            pl.BlockSpec((t, LANES), lambda b, p, i, rel: (b * nq + i, COL_MQ // LANES + p)),
            pl.BlockSpec((seq, LANES), lambda b, p, i, rel: (b, COL_MK // LANES + p)),
            pl.BlockSpec((seq, LANES), lambda b, p, i, rel: (b, COL_MV // LANES + p)),
            pl.BlockSpec((t, LANES), lambda b, p, i, rel: (b * nq + i, p)),
        ],
        out_specs=pl.BlockSpec((t, LANES), lambda b, p, i, rel: (b * nq + i, p)),
        scratch_shapes=[
            pltpu.VMEM((2, t, 2 * t), jnp.float32),
            pltpu.VMEM((seq, 2 * LANES), jnp.bfloat16),
            pltpu.VMEM((seq, 2 * LANES), jnp.bfloat16),
            pltpu.VMEM((2 * t, 2 * LANES), jnp.bfloat16),
            pltpu.VMEM((2 * t, t), jnp.float32),
            pltpu.VMEM((2 * t, t), jnp.float32),
            pltpu.VMEM((2 * t, LANES), jnp.float32),
            pltpu.VMEM((2 * t, 2 * LANES), jnp.float32),
        ],
    )
    return pl.pallas_call(
        _moba_attn_kernel,
        grid_spec=grid_spec,
        out_shape=jax.ShapeDtypeStruct((batch * seq, MOBA_HEADS * HEAD_DIM), jnp.bfloat16),
        compiler_params=_compiler_params(3),
        name="moba_attn",
    )(rel_t, qkv, qkv, qkv, neg)


def _merge_kernel(ad_ref, am_ref, gl_ref, bg_ref, h_ref, wbd_ref, wbm_ref, wo_ref,
                  g_ref, b_ref, o_ref):
    for rows in _row_chains(o_ref.shape[0]):
        y_d = jnp.dot(ad_ref[rows, :], wbd_ref[...], preferred_element_type=jnp.float32)
        y_m = jnp.dot(am_ref[rows, :], wbm_ref[...], preferred_element_type=jnp.float32)
        gates = jax.nn.sigmoid(gl_ref[rows, :] + bg_ref[...])
        mixed = gates[:, :D_MODEL] * y_d + gates[:, D_MODEL:] * y_m
        out = jnp.dot(mixed.astype(jnp.bfloat16), wo_ref[...], preferred_element_type=jnp.float32)
        o_ref[rows, :] = _layer_norm(DEEPNORM_ALPHA * h_ref[rows, :] + out, g_ref[...], b_ref[...])


def _merge(attn_d, attn_m, gl, b_gate, h, wbd, wbm, wout, ln_g, ln_b):
    n_rows = h.shape[0]
    tm = ROW_TILE
    row = lambda cols: pl.BlockSpec((tm, cols), lambda i: (i, 0))
    return pl.pallas_call(
        _merge_kernel,
        grid=(n_rows // tm,),
        in_specs=[row(D_MODEL), row(MOBA_HEADS * HEAD_DIM), row(2 * D_MODEL),
                  _resident((1, 2 * D_MODEL)), row(D_MODEL),
                  _resident(wbd.shape), _resident(wbm.shape), _resident(wout.shape),
                  _resident((1, D_MODEL)), _resident((1, D_MODEL))],
        out_specs=row(D_MODEL),
        out_shape=jax.ShapeDtypeStruct((n_rows, D_MODEL), jnp.float32),
        compiler_params=_compiler_params(1),
        name="merge",
    )(attn_d, attn_m, gl, b_gate, h, wbd, wbm, wout, ln_g, ln_b)


def _xattn_kv_kernel(mem_ref, wk_ref, wv_ref, k_ref, v_ref):
    mb = mem_ref[...].astype(jnp.bfloat16)
    k_ref[...] = jnp.dot(mb, wk_ref[...], preferred_element_type=jnp.float32).astype(k_ref.dtype)
    v_ref[...] = jnp.dot(mb, wv_ref[...], preferred_element_type=jnp.float32).astype(v_ref.dtype)


def _xattn_kv(mem2, wk, wv, batch, n_mem):
    blk = pl.BlockSpec((n_mem, D_MODEL), lambda b: (b, 0))
    return pl.pallas_call(
        _xattn_kv_kernel,
        grid=(batch,),
        in_specs=[blk, _resident(wk.shape), _resident(wv.shape)],
        out_specs=[blk, blk],
        out_shape=[jax.ShapeDtypeStruct(mem2.shape, jnp.bfloat16)] * 2,
        compiler_params=_compiler_params(1),
        name="xattn_kv",
    )(mem2, wk, wv)


def _xattn_kernel(h_ref, wq_ref, k_ref, v_ref, wo_ref, g_ref, b_ref, o_ref):
    for rows in _row_chains(o_ref.shape[0]):
        h = h_ref[rows, :]
        q = jnp.dot(h.astype(jnp.bfloat16), wq_ref[...], preferred_element_type=jnp.float32)
        q = (q * XATTN_HEAD_DIM ** -0.5).astype(jnp.bfloat16)
        outs = []
        for hd in range(XATTN_HEADS):
            cols = slice(hd * XATTN_HEAD_DIM, (hd + 1) * XATTN_HEAD_DIM)
            s = _qk(q[:, cols], k_ref[:, cols])
            p = jnp.exp(s - jnp.max(s, axis=1, keepdims=True))
            p = p / jnp.sum(p, axis=1, keepdims=True)
            outs.append(jnp.dot(p.astype(jnp.bfloat16), v_ref[:, cols],
                                preferred_element_type=jnp.float32))
        o = jnp.concatenate(outs, axis=1).astype(jnp.bfloat16)
        xa = jnp.dot(o, wo_ref[...], preferred_element_type=jnp.float32)
        o_ref[rows, :] = _layer_norm(DEEPNORM_ALPHA * h + xa, g_ref[...], b_ref[...])


def _xattn(h1, wq, kx, vx, wo, ln_g, ln_b, batch, seq, n_mem):
    tm = ROW_TILE
    nq = seq // tm
    row = pl.BlockSpec((tm, D_MODEL), lambda b, i: (b * nq + i, 0))
    kv = pl.BlockSpec((n_mem, D_MODEL), lambda b, i: (b, 0))
    return pl.pallas_call(
        _xattn_kernel,
        grid=(batch, nq),
        in_specs=[row, _resident(wq.shape), kv, kv, _resident(wo.shape),
                  _resident((1, D_MODEL)), _resident((1, D_MODEL))],
        out_specs=row,
        out_shape=jax.ShapeDtypeStruct(h1.shape, jnp.float32),
        compiler_params=_compiler_params(2),
        name="xattn",
    )(h1, wq, kx, vx, wo, ln_g, ln_b)


def _mlp_kernel(h_ref, w1_ref, w2_ref, g_ref, b_ref, o_ref, hb_ref, acc_ref):
    j = pl.program_id(1)

    @pl.when(j == 0)
    def _():
        hb_ref[...] = h_ref[...].astype(jnp.bfloat16)
        acc_ref[...] = jnp.zeros(acc_ref.shape, jnp.float32)

    for rows in _row_chains(o_ref.shape[0]):
        u = jnp.dot(hb_ref[rows, :], w1_ref[...], preferred_element_type=jnp.float32)
        u = jnp.square(jnp.maximum(u, 0.0)).astype(jnp.bfloat16)
        acc_ref[rows, :] += jnp.dot(u, w2_ref[...], preferred_element_type=jnp.float32)

    @pl.when(j == pl.num_programs(1) - 1)
    def _():
        o_ref[...] = _layer_norm(DEEPNORM_ALPHA * h_ref[...] + acc_ref[...],
                                 g_ref[...], b_ref[...])


def _mlp(h2, w1, w2, ln_g, ln_b):
    n_rows = h2.shape[0]
    tm = MLP_ROW_TILE
    row = pl.BlockSpec((tm, D_MODEL), lambda i, j: (i, 0))
    return pl.pallas_call(
        _mlp_kernel,
        grid=(n_rows // tm, D_FF // FF_TILE),
        in_specs=[row,
                  pl.BlockSpec((D_MODEL, FF_TILE), lambda i, j: (0, j)),
                  pl.BlockSpec((FF_TILE, D_MODEL), lambda i, j: (j, 0)),
                  _resident((1, D_MODEL)), _resident((1, D_MODEL))],
        out_specs=row,
        out_shape=jax.ShapeDtypeStruct(h2.shape, jnp.float32),
        scratch_shapes=[pltpu.VMEM((tm, D_MODEL), jnp.bfloat16),
                        pltpu.VMEM((tm, D_MODEL), jnp.float32)],
        compiler_params=_compiler_params(2),
        name="mlp",
    )(h2, w1, w2, ln_g, ln_b)


def kernel(x, mem, ln_in_g, ln_in_b, rel_table, w_in, b_gate, lam_q1, lam_k1, lam_q2, lam_k2,
           diff_sub_g, w_br_diff, w_br_moba, w_out, ln1_g, ln1_b, wq_x, wk_x, wv_x, wo_x,
           ln2_g, ln2_b, w_ff1, w_ff2, ln3_g, ln3_b):
    batch, seq, d = x.shape
    n_mem = mem.shape[1]
    assert d == D_MODEL and w_in.shape == (DEPTH, D_MODEL, W_IN_COLS)
    assert seq % ROW_TILE == 0 and seq % DIFF_TILE == 0 and seq % SEL_TILE == 0
    assert (batch * seq) % MLP_ROW_TILE == 0
    assert seq % MOBA_TILE == 0 and MOBA_TILE % MOBA_BLOCK == 0
    assert seq // MOBA_BLOCK <= LANES // 2 and MOBA_TILE >= REL_FAR_DIST <= DIFF_TILE
    bf = lambda w: w.astype(jnp.bfloat16)
    vec = lambda v: v.reshape(1, -1).astype(jnp.float32)
    l = 0

    x2 = x.reshape(batch * seq, d)
    h, qkv, mqf, gl, kmean = _ln_inproj(x2, vec(ln_in_g), vec(ln_in_b), bf(w_in[l]))

    n_blocks = seq // MOBA_BLOCK
    km = kmean.reshape(batch, n_blocks, MOBA_HEADS // 2, 2, HEAD_DIM)
    km = jnp.pad(km, ((0, 0), (0, LANES // 2 - n_blocks), (0, 0), (0, 0), (0, 0)))
    eye2 = jnp.eye(2, dtype=jnp.float32)
    km2 = jnp.einsum('bnpad,ac->bpcnad', km, eye2).reshape(batch, MOBA_HEADS // 2, LANES, LANES)
    neg = _moba_select(mqf, km2, batch, seq)

    rel_t = rel_table.T.astype(jnp.float32)
    lam_p = jnp.concatenate([lam_q1[l:l + 1], lam_k1[l:l + 1], lam_q2[l:l + 1], lam_k2[l:l + 1]],
                            axis=0).astype(jnp.float32)
    attn_d = _diff_attn(rel_t, qkv, lam_p, vec(diff_sub_g[l]), batch, seq)
    attn_m = _moba_attn(rel_t, qkv, neg, batch, seq)

    h1 = _merge(attn_d, attn_m, gl, vec(b_gate[l]), h, bf(w_br_diff[l]), bf(w_br_moba[l]),
                bf(w_out[l]), vec(ln1_g[l]), vec(ln1_b[l]))

    kx, vx = _xattn_kv(mem.reshape(batch * n_mem, d), bf(wk_x[l]), bf(wv_x[l]), batch, n_mem)
    h2 = _xattn(h1, bf(wq_x[l]), kx, vx, bf(wo_x[l]), vec(ln2_g[l]), vec(ln2_b[l]),
                batch, seq, n_mem)

    h3 = _mlp(h2, bf(w_ff1[l]), bf(w_ff2[l]), vec(ln3_g[l]), vec(ln3_b[l]))
    return h3.reshape(batch, seq, d)
```

```python
import functools
import math

import numpy as np
import jax
import jax.numpy as jnp
from jax import lax
from jax.experimental import pallas as pl
from jax.experimental.pallas import tpu as pltpu

D_MODEL = 1024
HEAD_DIM = 64
DIFF_HEADS = 8
MOBA_HEADS = 8
MOBA_BLOCK = 256
MOBA_TOPK = 3
XATTN_HEADS = 4
XATTN_HEAD_DIM = D_MODEL // XATTN_HEADS
D_FF = 4 * D_MODEL
REL_BUCKETS = 32
REL_MAX_DIST = 128
LN_EPS = 1e-5
DEPTH = 1
DEEPNORM_ALPHA = (2.0 * DEPTH) ** 0.25
LAM_INIT = 0.8 - 0.6 * math.exp(-0.3 * 0)
LOG2_E = math.log2(math.e)

COL_DQ = 0
COL_DK = 1024
COL_DV = 2048
COL_MQ = 3072
COL_MK = 3584
COL_MV = 4096
COL_GATE = 4608
N_QKV_COLS = 4608
W_IN_COLS = 6656

LANES = 128
VMEM_LIMIT_BYTES = 56 * 1024 * 1024

ROW_TILE = 512
DIFF_TILE = 512
MOBA_TILE = 512
SEL_TILE = 512
FF_TILE = 1024
PROJ_CHUNK = 512
FAR_UNROLL = 8
ROW_CHAINS = 2
MLP_ROW_TILE = 1024

MASK_NEG = -1e30


def _rel_thresholds():
    max_exact = REL_BUCKETS // 2
    n = np.arange(0, 4 * REL_MAX_DIST)
    nf = np.maximum(n, 1).astype(np.float32)
    large = max_exact + (np.log(nf / max_exact) / math.log(REL_MAX_DIST / max_exact)
                         * (REL_BUCKETS - max_exact)).astype(np.int32)
    large = np.minimum(large, REL_BUCKETS - 1)
    bucket = np.where(n < max_exact, n, large)
    thr = [int(np.argmax(bucket >= k)) for k in range(REL_BUCKETS)]
    assert all(bucket[t] == k for k, t in enumerate(thr))
    return thr


REL_THRESHOLDS = _rel_thresholds()
REL_FAR_DIST = REL_THRESHOLDS[-1]


def _layer_norm(z, g, b):
    mu = jnp.mean(z, axis=-1, keepdims=True)
    zc = z - mu
    var = jnp.mean(zc * zc, axis=-1, keepdims=True)
    return zc * lax.rsqrt(var + LN_EPS) * g + b


def _compiler_params(n_grid):
    return pltpu.CompilerParams(
        dimension_semantics=("arbitrary",) * n_grid,
        vmem_limit_bytes=VMEM_LIMIT_BYTES,
    )


def _row_chains(n_rows):
    step = n_rows // ROW_CHAINS
    return [pl.ds(r0, step) for r0 in range(0, n_rows, step)]


def _resident(shape):
    nd = len(shape)
    return pl.BlockSpec(shape, lambda *_: (0,) * nd, pipeline_mode=pl.Buffered(1))


def _ln_inproj_kernel(x_ref, g_ref, b_ref, w_ref, h_ref, qkv_ref, mqf_ref, gl_ref, kmean_ref):
    h = _layer_norm(x_ref[...], g_ref[...], b_ref[...])
    h_ref[...] = h
    hb = h.astype(jnp.bfloat16)
    q_scale = HEAD_DIM ** -0.5 * LOG2_E
    for c0 in range(0, W_IN_COLS, PROJ_CHUNK):
        acc = jnp.dot(hb, w_ref[:, c0:c0 + PROJ_CHUNK], preferred_element_type=jnp.float32)
        if c0 < COL_GATE:
            is_q = c0 < COL_DK or COL_MQ <= c0 < COL_MK
            if COL_MQ <= c0 < COL_MK:
                mqf_ref[:, c0 - COL_MQ:c0 - COL_MQ + PROJ_CHUNK] = acc
            if COL_MK <= c0 < COL_MV:
                nblk = acc.shape[0] // MOBA_BLOCK
                km = jnp.mean(acc.reshape(nblk, MOBA_BLOCK, PROJ_CHUNK), axis=1)
                kmean_ref[0, :, c0 - COL_MK:c0 - COL_MK + PROJ_CHUNK] = km
            val = acc * q_scale if is_q else acc
            qkv_ref[:, c0:c0 + PROJ_CHUNK] = val.astype(jnp.bfloat16)
        else:
            gl_ref[:, c0 - COL_GATE:c0 - COL_GATE + PROJ_CHUNK] = acc


def _ln_inproj(x2, ln_g, ln_b, w_in_bf):
    n_rows = x2.shape[0]
    tm = ROW_TILE
    nblk = tm // MOBA_BLOCK
    grid = (n_rows // tm,)
    row = lambda cols: pl.BlockSpec((tm, cols), lambda i: (i, 0))
    return pl.pallas_call(
        _ln_inproj_kernel,
        grid=grid,
        in_specs=[row(D_MODEL), _resident((1, D_MODEL)), _resident((1, D_MODEL)),
                  _resident((D_MODEL, W_IN_COLS))],
        out_specs=[row(D_MODEL), row(N_QKV_COLS), row(MOBA_HEADS * HEAD_DIM), row(2 * D_MODEL),
                   pl.BlockSpec((1, nblk, MOBA_HEADS * HEAD_DIM), lambda i: (i, 0, 0))],
        out_shape=[
            jax.ShapeDtypeStruct((n_rows, D_MODEL), jnp.float32),
            jax.ShapeDtypeStruct((n_rows, N_QKV_COLS), jnp.bfloat16),
            jax.ShapeDtypeStruct((n_rows, MOBA_HEADS * HEAD_DIM), jnp.float32),
            jax.ShapeDtypeStruct((n_rows, 2 * D_MODEL), jnp.float32),
            jax.ShapeDtypeStruct((n_rows // tm, nblk, MOBA_HEADS * HEAD_DIM), jnp.float32),
        ],
        compiler_params=_compiler_params(1),
        name="ln_inproj",
    )(x2, ln_g, ln_b, w_in_bf)


def _moba_select_kernel(q_ref, km_ref, neg_ref, *, n_blocks):
    t = q_ref.shape[0]
    half = LANES // 2
    row0 = pl.program_id(1) * t
    shape3 = (2, half, t)
    blk = lax.broadcasted_iota(jnp.int32, shape3, 1)
    cur = (row0 + lax.broadcasted_iota(jnp.int32, shape3, 2)) // MOBA_BLOCK
    blk_f = blk.astype(jnp.float32)
    ninf = jnp.float32(-jnp.inf)
    for pair in range(MOBA_HEADS // 2):
        q = q_ref[:, pair * LANES:(pair + 1) * LANES]
        gate = lax.dot_general(km_ref[0, pair], q, (((1,), (1,)), ((), ())),
                               preferred_element_type=jnp.float32,
                               precision=lax.Precision.HIGHEST)
        g = jnp.where((blk < cur) & (blk < n_blocks), gate.reshape(shape3), ninf)
        neg = jnp.full(shape3, MASK_NEG, jnp.float32)
        for rank in range(MOBA_TOPK):
            mx = jnp.max(g, axis=1, keepdims=True)
            ix = jnp.min(jnp.where(g == mx, blk_f, float(half)), axis=1, keepdims=True)
            pick = blk_f == ix
            neg = jnp.where(pick & (rank < cur), 0.0, neg)
            g = jnp.where(pick, ninf, g)
        neg_ref[:, pair * LANES:(pair + 1) * LANES] = neg.reshape(LANES, t).T


def _moba_select(mqf, km2, batch, seq):
    t = SEL_TILE
    nq = seq // t
    n_blocks = seq // MOBA_BLOCK
    return pl.pallas_call(
        functools.partial(_moba_select_kernel, n_blocks=n_blocks),
        grid=(batch, nq),
        in_specs=[pl.BlockSpec((t, MOBA_HEADS * HEAD_DIM), lambda b, i: (b * nq + i, 0)),
                  pl.BlockSpec((1, MOBA_HEADS // 2, LANES, LANES), lambda b, i: (b, 0, 0, 0))],
        out_specs=pl.BlockSpec((t, MOBA_HEADS * HEAD_DIM), lambda b, i: (b * nq + i, 0)),
        out_shape=jax.ShapeDtypeStruct(mqf.shape, jnp.float32),
        compiler_params=_compiler_params(2),
        name="moba_select",
    )(mqf, km2)


def _fill_bias_table(tab_ref, rel_ref, head, t):
    far = rel_ref[head, REL_BUCKETS - 1]
    r = lax.broadcasted_iota(jnp.int32, (t, 2 * t), 0)
    c = lax.broadcasted_iota(jnp.int32, (t, 2 * t), 1)
    dist = r - c + t
    bias = jnp.full((t, 2 * t), (rel_ref[head, 0] - far) * LOG2_E, jnp.float32)
    for k in range(1, REL_BUCKETS):
        bias = jnp.where(dist >= REL_THRESHOLDS[k], (rel_ref[head, k] - far) * LOG2_E, bias)
    tab_ref[...] = jnp.where(dist >= 0, bias, MASK_NEG)


def _softmax_step(s, v_aug, m_ref, acc_ref):
    m_prev = m_ref[...]
    m_new = jnp.maximum(m_prev, jnp.max(s, axis=1, keepdims=True))
    alpha = jnp.exp2(m_prev - m_new)
    p = jnp.exp2(s - jnp.tile(m_new, (1, s.shape[1] // LANES)))
    pv = jnp.dot(p.astype(jnp.bfloat16), v_aug, preferred_element_type=jnp.float32)
    acc_ref[...] = jnp.tile(alpha, (1, 2)) * acc_ref[...] + pv
    m_ref[...] = m_new


def _far_tiles(n, logits_of, values_of, sa_ref, sb_ref, m_ref, acc_ref):
    def run(first, count):
        for u in range(count):
            cur, nxt = (sa_ref, sb_ref) if u % 2 == 0 else (sb_ref, sa_ref)
            nxt[...] = logits_of(jnp.minimum(first + u + 1, n - 1))
            _softmax_step(cur[...], values_of(first + u), m_ref, acc_ref)

    groups = lax.shift_right_logical(n, FAR_UNROLL.bit_length() - 1)

    def group(i, carry):
        run(i * FAR_UNROLL, FAR_UNROLL)
        return carry

    lax.fori_loop(0, groups, group, 0)
    done = groups * FAR_UNROLL
    for count in (FAR_UNROLL // 2, FAR_UNROLL // 4):
        if count >= 2:
            @pl.when(((n - done) & count) != 0)
            def _(done=done, count=count):
                run(done, count)
            done = done + ((n - done) & count)

    @pl.when((n & 1) == 1)
    def _():
        _softmax_step(sa_ref[...], values_of(n - 1), m_ref, acc_ref)


def _init_softmax(m_ref, acc_ref):
    m_ref[...] = jnp.full(m_ref.shape, MASK_NEG, jnp.float32)
    acc_ref[...] = jnp.zeros(acc_ref.shape, jnp.float32)


def _fill_values_aug(vaug_ref, v_ref):
    vaug_ref[:, 0:LANES] = v_ref[...]
    vaug_ref[:, LANES:2 * LANES] = jnp.ones(v_ref.shape, vaug_ref.dtype)


def _qk(q2, k):
    return lax.dot_general(q2, k, (((1,), (1,)), ((), ())), preferred_element_type=jnp.float32)


def _split_halves(q):
    lane = lax.broadcasted_iota(jnp.int32, q.shape, 1)
    zero = jnp.zeros_like(q)
    return jnp.concatenate([jnp.where(lane < LANES // 2, q, zero),
                            jnp.where(lane < LANES // 2, zero, q)], axis=0)


def _diff_attn_kernel(rel_ref, q_ref, k_ref, v_ref, lam_ref, subg_ref, o_ref,
                      tab_ref, vaug_ref, sa_ref, sb_ref, sd_ref, m_ref, acc_ref):
    t = DIFF_TILE
    n_q = q_ref.shape[0] // t
    head = pl.program_id(1)
    _fill_bias_table(tab_ref, rel_ref, head, t)
    _fill_values_aug(vaug_ref, v_ref)
    lam_p = lam_ref[...]
    lam = (jnp.exp(jnp.sum(lam_p[0:1] * lam_p[1:2], axis=1, keepdims=True))
           - jnp.exp(jnp.sum(lam_p[2:3] * lam_p[3:4], axis=1, keepdims=True)) + LAM_INIT)

    def rows_of(i):
        return pl.ds(pl.multiple_of(i * t, t), t)

    def queries_of(i):
        return _split_halves(q_ref[rows_of(i), :])

    def own_logits(i):
        q2 = queries_of(i)
        k_own = k_ref[rows_of(i), :]
        sd_ref[0:t, :] = _qk(q2[0:t], k_own) + tab_ref[:, t:2 * t]
        sd_ref[t:2 * t, :] = _qk(q2[t:2 * t], k_own) + tab_ref[:, t:2 * t]

    own_logits(0)

    def query_tile(qi, carry):
        q2 = queries_of(qi)

        def logits_of(j):
            return _qk(q2, k_ref[rows_of(j), :])

        def values_of(j):
            return vaug_ref[rows_of(j), :]

        prev = jnp.maximum(qi - 1, 0)
        _init_softmax(m_ref, acc_ref)
        sb_ref[...] = (logits_of(prev).reshape(2, t, t) + tab_ref[:, 0:t][None]).reshape(2 * t, t)
        _softmax_step(sd_ref[...], values_of(qi), m_ref, acc_ref)

        @pl.when(qi >= 1)
        def _():
            sa_ref[...] = logits_of(0)
            _softmax_step(sb_ref[...], values_of(prev), m_ref, acc_ref)

        _far_tiles(jnp.maximum(qi - 1, 0), logits_of, values_of, sa_ref, sb_ref, m_ref, acc_ref)

        own_logits(jnp.minimum(qi + 1, n_q - 1))
        o = acc_ref[:, 0:LANES] / acc_ref[:, LANES:2 * LANES]
        o = o[:t] - lam * o[t:]
        o = o * lax.rsqrt(jnp.mean(o * o, axis=1, keepdims=True) + LN_EPS) * subg_ref[...]
        o_ref[rows_of(qi), :] = (o * (1.0 - LAM_INIT)).astype(o_ref.dtype)
        return carry

    lax.fori_loop(0, n_q, query_tile, 0)


def _diff_attn(rel_t, qkv, lam_p, sub_g, batch, seq):
    t = DIFF_TILE
    once = pl.Buffered(1)
    grid_spec = pltpu.PrefetchScalarGridSpec(
        num_scalar_prefetch=1,
        grid=(batch, DIFF_HEADS),
        in_specs=[
            pl.BlockSpec((seq, LANES), lambda b, h, rel: (b, COL_DQ // LANES + h)),
            pl.BlockSpec((seq, LANES), lambda b, h, rel: (b, COL_DK // LANES + h)),
            pl.BlockSpec((seq, LANES), lambda b, h, rel: (b, COL_DV // LANES + h),
                         pipeline_mode=once),
            pl.BlockSpec((4, HEAD_DIM), lambda b, h, rel: (0, 0)),
            pl.BlockSpec((1, LANES), lambda b, h, rel: (0, 0)),
        ],
        out_specs=pl.BlockSpec((seq, LANES), lambda b, h, rel: (b, h)),
        scratch_shapes=[
            pltpu.VMEM((t, 2 * t), jnp.float32),
            pltpu.VMEM((seq, 2 * LANES), jnp.bfloat16),
            pltpu.VMEM((2 * t, t), jnp.float32),
            pltpu.VMEM((2 * t, t), jnp.float32),
            pltpu.VMEM((2 * t, t), jnp.float32),
            pltpu.VMEM((2 * t, LANES), jnp.float32),
            pltpu.VMEM((2 * t, 2 * LANES), jnp.float32),
        ],
    )
    return pl.pallas_call(
        _diff_attn_kernel,
        grid_spec=grid_spec,
        out_shape=jax.ShapeDtypeStruct((batch * seq, DIFF_HEADS * LANES), jnp.bfloat16),
        compiler_params=_compiler_params(2),
        name="diff_attn",
    )(rel_t, qkv, qkv, qkv, lam_p, sub_g)


def _moba_attn_kernel(rel_ref, q_ref, k_ref, v_ref, neg_ref, o_ref,
                      tab_ref, kaug_ref, vaug_ref, lhs_ref, sa_ref, sb_ref, m_ref, acc_ref):
    t = q_ref.shape[0]
    seq = k_ref.shape[0]
    half = LANES // 2
    blocks_per_tile = t // MOBA_BLOCK
    pair = pl.program_id(1)
    qi = pl.program_id(2)

    @pl.when(qi == 0)
    def _():
        _fill_bias_table(tab_ref.at[0], rel_ref, DIFF_HEADS + 2 * pair, t)
        _fill_bias_table(tab_ref.at[1], rel_ref, DIFF_HEADS + 2 * pair + 1, t)
        _fill_values_aug(vaug_ref, v_ref)
        kaug_ref[:, 0:LANES] = k_ref[...]
        key_blk = lax.broadcasted_iota(jnp.int32, (seq, LANES), 0) // MOBA_BLOCK
        lane_k = lax.broadcasted_iota(jnp.int32, (seq, LANES), 1)
        kaug_ref[:, LANES:2 * LANES] = jnp.where(key_blk == lane_k, 1.0, 0.0).astype(kaug_ref.dtype)

    q2 = _split_halves(q_ref[...])
    neg = neg_ref[...]
    neg2 = jnp.concatenate([neg, pltpu.roll(neg, half, axis=1)], axis=0)
    lhs_ref[:, 0:LANES] = q2
    lhs_ref[:, LANES:2 * LANES] = neg2.astype(lhs_ref.dtype)

    def keys_of(j):
        return kaug_ref[pl.ds(pl.multiple_of(j * t, t), t), :]

    def logits_of(j):
        return _qk(lhs_ref[...], keys_of(j))

    def values_of(j):
        return vaug_ref[pl.ds(pl.multiple_of(j * t, t), t), :]

    prev = jnp.maximum(qi - 1, 0)
    _init_softmax(m_ref, acc_ref)
    k_own = keys_of(qi)[:, 0:LANES]
    s = jnp.stack([_qk(q2[0:t], k_own), _qk(q2[t:2 * t], k_own)]) + tab_ref[:, :, t:2 * t]
    row_blk = lax.broadcasted_iota(jnp.int32, (2, t, t), 1) // MOBA_BLOCK
    col_blk = lax.broadcasted_iota(jnp.int32, (2, t, t), 2) // MOBA_BLOCK
    lane = lax.broadcasted_iota(jnp.int32, (2 * t, LANES), 1)
    for cb in range(blocks_per_tile - 1):
        chosen = jnp.sum(jnp.where(lane == qi * blocks_per_tile + cb, neg2, 0.0),
                         axis=1, keepdims=True).reshape(2, t, 1)
        s = s + jnp.where((col_blk == cb) & (row_blk > cb), chosen, 0.0)
    sa_ref[...] = s.reshape(2 * t, t)
    sb_ref[...] = (logits_of(prev).reshape(2, t, t) + tab_ref[:, :, 0:t]).reshape(2 * t, t)
    _softmax_step(sa_ref[...], values_of(qi), m_ref, acc_ref)

    @pl.when(qi >= 1)
    def _():
        sa_ref[...] = logits_of(0)
        _softmax_step(sb_ref[...], values_of(prev), m_ref, acc_ref)

    _far_tiles(jnp.maximum(qi - 1, 0), logits_of, values_of, sa_ref, sb_ref, m_ref, acc_ref)

    o = acc_ref[:, 0:LANES] / acc_ref[:, LANES:2 * LANES]
    lane_o = lax.broadcasted_iota(jnp.int32, (t, LANES), 1)
    o_ref[...] = jnp.where(lane_o < half, o[:t], o[t:]).astype(o_ref.dtype)


def _moba_attn(rel_t, qkv, neg, batch, seq):
    t = MOBA_TILE
    nq = seq // t
    n_pairs = MOBA_HEADS // 2
    grid_spec = pltpu.PrefetchScalarGridSpec(
        num_scalar_prefetch=1,
        grid=(batch, n_pairs, nq),
        in_specs=[
            pl.BlockSpec((t, LANES), lambda b, p, i, rel: (b * nq + i, COL_MQ // LANES + p)),
            pl.BlockSpec((seq, LANES), lambda b, p, i, rel: (b, COL_MK // LANES + p)),
            pl.BlockSpec((seq, LANES), lambda b, p, i, rel: (b, COL_MV // LANES + p)),
            pl.BlockSpec((t, LANES), lambda b, p, i, rel: (b * nq + i, p)),
        ],
        out_specs=pl.BlockSpec((t, LANES), lambda b, p, i, rel: (b * nq + i, p)),
        scratch_shapes=[
            pltpu.VMEM((2, t, 2 * t), jnp.float32),
            pltpu.VMEM((seq, 2 * LANES), jnp.bfloat16),
            pltpu.VMEM((seq, 2 * LANES), jnp.bfloat16),
            pltpu.VMEM((2 * t, 2 * LANES), jnp.bfloat16),
            pltpu.VMEM((2 * t, t), jnp.float32),
            pltpu.VMEM((2 * t, t), jnp.float32),
            pltpu.VMEM((2 * t, LANES), jnp.float32),
            pltpu.VMEM((2 * t, 2 * LANES), jnp.float32),
        ],
    )
    return pl.pallas_call(
        _moba_attn_kernel,
        grid_spec=grid_spec,
        out_shape=jax.ShapeDtypeStruct((batch * seq, MOBA_HEADS * HEAD_DIM), jnp.bfloat16),
        compiler_params=_compiler_params(3),
        name="moba_attn",
    )(rel_t, qkv, qkv, qkv, neg)


def _merge_kernel(ad_ref, am_ref, gl_ref, bg_ref, h_ref, wbd_ref, wbm_ref, wo_ref,
                  g_ref, b_ref, o_ref):
    for rows in _row_chains(o_ref.shape[0]):
        y_d = jnp.dot(ad_ref[rows, :], wbd_ref[...], preferred_element_type=jnp.float32)
        y_m = jnp.dot(am_ref[rows, :], wbm_ref[...], preferred_element_type=jnp.float32)
        gates = jax.nn.sigmoid(gl_ref[rows, :] + bg_ref[...])
        mixed = gates[:, :D_MODEL] * y_d + gates[:, D_MODEL:] * y_m
        out = jnp.dot(mixed.astype(jnp.bfloat16), wo_ref[...], preferred_element_type=jnp.float32)
        o_ref[rows, :] = _layer_norm(DEEPNORM_ALPHA * h_ref[rows, :] + out, g_ref[...], b_ref[...])


def _merge(attn_d, attn_m, gl, b_gate, h, wbd, wbm, wout, ln_g, ln_b):
    n_rows = h.shape[0]
    tm = ROW_TILE
    row = lambda cols: pl.BlockSpec((tm, cols), lambda i: (i, 0))
    return pl.pallas_call(
        _merge_kernel,
        grid=(n_rows // tm,),
        in_specs=[row(D_MODEL), row(MOBA_HEADS * HEAD_DIM), row(2 * D_MODEL),
                  _resident((1, 2 * D_MODEL)), row(D_MODEL),
                  _resident(wbd.shape), _resident(wbm.shape), _resident(wout.shape),
                  _resident((1, D_MODEL)), _resident((1, D_MODEL))],
        out_specs=row(D_MODEL),
        out_shape=jax.ShapeDtypeStruct((n_rows, D_MODEL), jnp.float32),
        compiler_params=_compiler_params(1),
        name="merge",
    )(attn_d, attn_m, gl, b_gate, h, wbd, wbm, wout, ln_g, ln_b)


def _xattn_kv_kernel(mem_ref, wk_ref, wv_ref, k_ref, v_ref):
    mb = mem_ref[...].astype(jnp.bfloat16)
    k_ref[...] = jnp.dot(mb, wk_ref[...], preferred_element_type=jnp.float32).astype(k_ref.dtype)
    v_ref[...] = jnp.dot(mb, wv_ref[...], preferred_element_type=jnp.float32).astype(v_ref.dtype)


def _xattn_kv(mem2, wk, wv, batch, n_mem):
    blk = pl.BlockSpec((n_mem, D_MODEL), lambda b: (b, 0))
    return pl.pallas_call(
        _xattn_kv_kernel,
        grid=(batch,),
        in_specs=[blk, _resident(wk.shape), _resident(wv.shape)],
        out_specs=[blk, blk],
        out_shape=[jax.ShapeDtypeStruct(mem2.shape, jnp.bfloat16)] * 2,
        compiler_params=_compiler_params(1),
        name="xattn_kv",
    )(mem2, wk, wv)


def _xattn_kernel(h_ref, wq_ref, k_ref, v_ref, wo_ref, g_ref, b_ref, o_ref):
    for rows in _row_chains(o_ref.shape[0]):
        h = h_ref[rows, :]
        q = jnp.dot(h.astype(jnp.bfloat16), wq_ref[...], preferred_element_type=jnp.float32)
        q = (q * XATTN_HEAD_DIM ** -0.5).astype(jnp.bfloat16)
        outs = []
        for hd in range(XATTN_HEADS):
            cols = slice(hd * XATTN_HEAD_DIM, (hd + 1) * XATTN_HEAD_DIM)
            s = _qk(q[:, cols], k_ref[:, cols])
            p = jnp.exp(s - jnp.max(s, axis=1, keepdims=True))
            p = p / jnp.sum(p, axis=1, keepdims=True)
            outs.append(jnp.dot(p.astype(jnp.bfloat16), v_ref[:, cols],
                                preferred_element_type=jnp.float32))
        o = jnp.concatenate(outs, axis=1).astype(jnp.bfloat16)
        xa = jnp.dot(o, wo_ref[...], preferred_element_type=jnp.float32)
        o_ref[rows, :] = _layer_norm(DEEPNORM_ALPHA * h + xa, g_ref[...], b_ref[...])


def _xattn(h1, wq, kx, vx, wo, ln_g, ln_b, batch, seq, n_mem):
    tm = ROW_TILE
    nq = seq // tm
    row = pl.BlockSpec((tm, D_MODEL), lambda b, i: (b * nq + i, 0))
    kv = pl.BlockSpec((n_mem, D_MODEL), lambda b, i: (b, 0))
    return pl.pallas_call(
        _xattn_kernel,
        grid=(batch, nq),
        in_specs=[row, _resident(wq.shape), kv, kv, _resident(wo.shape),
                  _resident((1, D_MODEL)), _resident((1, D_MODEL))],
        out_specs=row,
        out_shape=jax.ShapeDtypeStruct(h1.shape, jnp.float32),
        compiler_params=_compiler_params(2),
        name="xattn",
    )(h1, wq, kx, vx, wo, ln_g, ln_b)


def _mlp_kernel(h_ref, w1_ref, w2_ref, g_ref, b_ref, o_ref, hb_ref, acc_ref):
    j = pl.program_id(1)

    @pl.when(j == 0)
    def _():
        hb_ref[...] = h_ref[...].astype(jnp.bfloat16)
        acc_ref[...] = jnp.zeros(acc_ref.shape, jnp.float32)

    for rows in _row_chains(o_ref.shape[0]):
        u = jnp.dot(hb_ref[rows, :], w1_ref[...], preferred_element_type=jnp.float32)
        u = jnp.square(jnp.maximum(u, 0.0)).astype(jnp.bfloat16)
        acc_ref[rows, :] += jnp.dot(u, w2_ref[...], preferred_element_type=jnp.float32)

    @pl.when(j == pl.num_programs(1) - 1)
    def _():
        o_ref[...] = _layer_norm(DEEPNORM_ALPHA * h_ref[...] + acc_ref[...],
                                 g_ref[...], b_ref[...])


def _mlp(h2, w1, w2, ln_g, ln_b):
    n_rows = h2.shape[0]
    tm = MLP_ROW_TILE
    row = pl.BlockSpec((tm, D_MODEL), lambda i, j: (i, 0))
    return pl.pallas_call(
        _mlp_kernel,
        grid=(n_rows // tm, D_FF // FF_TILE),
        in_specs=[row,
                  pl.BlockSpec((D_MODEL, FF_TILE), lambda i, j: (0, j)),
                  pl.BlockSpec((FF_TILE, D_MODEL), lambda i, j: (j, 0)),
                  _resident((1, D_MODEL)), _resident((1, D_MODEL))],
        out_specs=row,
        out_shape=jax.ShapeDtypeStruct(h2.shape, jnp.float32),
        scratch_shapes=[pltpu.VMEM((tm, D_MODEL), jnp.bfloat16),
                        pltpu.VMEM((tm, D_MODEL), jnp.float32)],
        compiler_params=_compiler_params(2),
        name="mlp",
    )(h2, w1, w2, ln_g, ln_b)


def kernel(x, mem, ln_in_g, ln_in_b, rel_table, w_in, b_gate, lam_q1, lam_k1, lam_q2, lam_k2,
           diff_sub_g, w_br_diff, w_br_moba, w_out, ln1_g, ln1_b, wq_x, wk_x, wv_x, wo_x,
           ln2_g, ln2_b, w_ff1, w_ff2, ln3_g, ln3_b):
    batch, seq, d = x.shape
    n_mem = mem.shape[1]
    assert d == D_MODEL and w_in.shape == (DEPTH, D_MODEL, W_IN_COLS)
    assert seq % ROW_TILE == 0 and seq % DIFF_TILE == 0 and seq % SEL_TILE == 0
    assert (batch * seq) % MLP_ROW_TILE == 0
    assert seq % MOBA_TILE == 0 and MOBA_TILE % MOBA_BLOCK == 0
    assert seq // MOBA_BLOCK <= LANES // 2 and MOBA_TILE >= REL_FAR_DIST <= DIFF_TILE
    bf = lambda w: w.astype(jnp.bfloat16)
    vec = lambda v: v.reshape(1, -1).astype(jnp.float32)
    l = 0

    x2 = x.reshape(batch * seq, d)
    h, qkv, mqf, gl, kmean = _ln_inproj(x2, vec(ln_in_g), vec(ln_in_b), bf(w_in[l]))

    n_blocks = seq // MOBA_BLOCK
    km = kmean.reshape(batch, n_blocks, MOBA_HEADS // 2, 2, HEAD_DIM)
    km = jnp.pad(km, ((0, 0), (0, LANES // 2 - n_blocks), (0, 0), (0, 0), (0, 0)))
    eye2 = jnp.eye(2, dtype=jnp.float32)
    km2 = jnp.einsum('bnpad,ac->bpcnad', km, eye2).reshape(batch, MOBA_HEADS // 2, LANES, LANES)
    neg = _moba_select(mqf, km2, batch, seq)

    rel_t = rel_table.T.astype(jnp.float32)
    lam_p = jnp.concatenate([lam_q1[l:l + 1], lam_k1[l:l + 1], lam_q2[l:l + 1], lam_k2[l:l + 1]],
                            axis=0).astype(jnp.float32)
    attn_d = _diff_attn(rel_t, qkv, lam_p, vec(diff_sub_g[l]), batch, seq)
    attn_m = _moba_attn(rel_t, qkv, neg, batch, seq)

    h1 = _merge(attn_d, attn_m, gl, vec(b_gate[l]), h, bf(w_br_diff[l]), bf(w_br_moba[l]),
                bf(w_out[l]), vec(ln1_g[l]), vec(ln1_b[l]))

    kx, vx = _xattn_kv(mem.reshape(batch * n_mem, d), bf(wk_x[l]), bf(wv_x[l]), batch, n_mem)
    h2 = _xattn(h1, bf(wq_x[l]), kx, vx, bf(wo_x[l]), vec(ln2_g[l]), vec(ln2_b[l]),
                batch, seq, n_mem)

    h3 = _mlp(h2, bf(w_ff1[l]), bf(w_ff2[l]), vec(ln3_g[l]), vec(ln3_b[l]))
    return h3.reshape(batch, seq, d)
```

```python
import functools
import math

import numpy as np
import jax
import jax.numpy as jnp
from jax import lax
from jax.experimental import pallas as pl
from jax.experimental.pallas import tpu as pltpu

D_MODEL = 1024
HEAD_DIM = 64
DIFF_HEADS = 8
MOBA_HEADS = 8
MOBA_BLOCK = 256
MOBA_TOPK = 3
XATTN_HEADS = 4
XATTN_HEAD_DIM = D_MODEL // XATTN_HEADS
D_FF = 4 * D_MODEL
REL_BUCKETS = 32
REL_MAX_DIST = 128
LN_EPS = 1e-5
DEPTH = 1
DEEPNORM_ALPHA = (2.0 * DEPTH) ** 0.25
LAM_INIT = 0.8 - 0.6 * math.exp(-0.3 * 0)
LOG2_E = math.log2(math.e)

COL_DQ = 0
COL_DK = 1024
COL_DV = 2048
COL_MQ = 3072
COL_MK = 3584
COL_MV = 4096
COL_GATE = 4608
N_QKV_COLS = 4608
W_IN_COLS = 6656

LANES = 128
VMEM_LIMIT_BYTES = 56 * 1024 * 1024

ROW_TILE = 512
DIFF_TILE = 512
MOBA_TILE = 512
SEL_TILE = 512
FF_TILE = 1024
PROJ_CHUNK = 512
FAR_UNROLL = 16
ROW_CHAINS = 2
MLP_ROW_TILE = 1024

MASK_NEG = -1e30


def _rel_thresholds():
    max_exact = REL_BUCKETS // 2
    n = np.arange(0, 4 * REL_MAX_DIST)
    nf = np.maximum(n, 1).astype(np.float32)
    large = max_exact + (np.log(nf / max_exact) / math.log(REL_MAX_DIST / max_exact)
                         * (REL_BUCKETS - max_exact)).astype(np.int32)
    large = np.minimum(large, REL_BUCKETS - 1)
    bucket = np.where(n < max_exact, n, large)
    thr = [int(np.argmax(bucket >= k)) for k in range(REL_BUCKETS)]
    assert all(bucket[t] == k for k, t in enumerate(thr))
    return thr


REL_THRESHOLDS = _rel_thresholds()
REL_FAR_DIST = REL_THRESHOLDS[-1]


def _layer_norm(z, g, b):
    mu = jnp.mean(z, axis=-1, keepdims=True)
    zc = z - mu
    var = jnp.mean(zc * zc, axis=-1, keepdims=True)
    return zc * lax.rsqrt(var + LN_EPS) * g + b


def _compiler_params(n_grid):
    return pltpu.CompilerParams(
        dimension_semantics=("arbitrary",) * n_grid,
        vmem_limit_bytes=VMEM_LIMIT_BYTES,
    )


def _row_chains(n_rows):
    step = n_rows // ROW_CHAINS
    return [pl.ds(r0, step) for r0 in range(0, n_rows, step)]


def _resident(shape):
    nd = len(shape)
    return pl.BlockSpec(shape, lambda *_: (0,) * nd, pipeline_mode=pl.Buffered(1))


def _ln_inproj_kernel(x_ref, g_ref, b_ref, w_ref, h_ref, qkv_ref, mqf_ref, gl_ref, kmean_ref):
    h = _layer_norm(x_ref[...], g_ref[...], b_ref[...])
    h_ref[...] = h
    hb = h.astype(jnp.bfloat16)
    q_scale = HEAD_DIM ** -0.5 * LOG2_E
    for c0 in range(0, W_IN_COLS, PROJ_CHUNK):
        acc = jnp.dot(hb, w_ref[:, c0:c0 + PROJ_CHUNK], preferred_element_type=jnp.float32)
        if c0 < COL_GATE:
            is_q = c0 < COL_DK or COL_MQ <= c0 < COL_MK
            if COL_MQ <= c0 < COL_MK:
                mqf_ref[:, c0 - COL_MQ:c0 - COL_MQ + PROJ_CHUNK] = acc
            if COL_MK <= c0 < COL_MV:
                nblk = acc.shape[0] // MOBA_BLOCK
                km = jnp.mean(acc.reshape(nblk, MOBA_BLOCK, PROJ_CHUNK), axis=1)
                kmean_ref[0, :, c0 - COL_MK:c0 - COL_MK + PROJ_CHUNK] = km
            val = acc * q_scale if is_q else acc
            qkv_ref[:, c0:c0 + PROJ_CHUNK] = val.astype(jnp.bfloat16)
        else:
            gl_ref[:, c0 - COL_GATE:c0 - COL_GATE + PROJ_CHUNK] = acc


def _ln_inproj(x2, ln_g, ln_b, w_in_bf):
    n_rows = x2.shape[0]
    tm = ROW_TILE
    nblk = tm // MOBA_BLOCK
    grid = (n_rows // tm,)
    row = lambda cols: pl.BlockSpec((tm, cols), lambda i: (i, 0))
    return pl.pallas_call(
        _ln_inproj_kernel,
        grid=grid,
        in_specs=[row(D_MODEL), _resident((1, D_MODEL)), _resident((1, D_MODEL)),
                  _resident((D_MODEL, W_IN_COLS))],
        out_specs=[row(D_MODEL), row(N_QKV_COLS), row(MOBA_HEADS * HEAD_DIM), row(2 * D_MODEL),
                   pl.BlockSpec((1, nblk, MOBA_HEADS * HEAD_DIM), lambda i: (i, 0, 0))],
        out_shape=[
            jax.ShapeDtypeStruct((n_rows, D_MODEL), jnp.float32),
            jax.ShapeDtypeStruct((n_rows, N_QKV_COLS), jnp.bfloat16),
            jax.ShapeDtypeStruct((n_rows, MOBA_HEADS * HEAD_DIM), jnp.float32),
            jax.ShapeDtypeStruct((n_rows, 2 * D_MODEL), jnp.float32),
            jax.ShapeDtypeStruct((n_rows // tm, nblk, MOBA_HEADS * HEAD_DIM), jnp.float32),
        ],
        compiler_params=_compiler_params(1),
        name="ln_inproj",
    )(x2, ln_g, ln_b, w_in_bf)


def _moba_select_kernel(q_ref, km_ref, neg_ref, *, n_blocks):
    t = q_ref.shape[0]
    half = LANES // 2
    row0 = pl.program_id(1) * t
    shape3 = (2, half, t)
    blk = lax.broadcasted_iota(jnp.int32, shape3, 1)
    cur = (row0 + lax.broadcasted_iota(jnp.int32, shape3, 2)) // MOBA_BLOCK
    blk_f = blk.astype(jnp.float32)
    ninf = jnp.float32(-jnp.inf)
    for pair in range(MOBA_HEADS // 2):
        q = q_ref[:, pair * LANES:(pair + 1) * LANES]
        gate = lax.dot_general(km_ref[0, pair], q, (((1,), (1,)), ((), ())),
                               preferred_element_type=jnp.float32,
                               precision=lax.Precision.HIGHEST)
        g = jnp.where((blk < cur) & (blk < n_blocks), gate.reshape(shape3), ninf)
        neg = jnp.full(shape3, MASK_NEG, jnp.float32)
        for rank in range(MOBA_TOPK):
            mx = jnp.max(g, axis=1, keepdims=True)
            ix = jnp.min(jnp.where(g == mx, blk_f, float(half)), axis=1, keepdims=True)
            pick = blk_f == ix
            neg = jnp.where(pick & (rank < cur), 0.0, neg)
            g = jnp.where(pick, ninf, g)
        neg_ref[:, pair * LANES:(pair + 1) * LANES] = neg.reshape(LANES, t).T


def _moba_select(mqf, km2, batch, seq):
    t = SEL_TILE
    nq = seq // t
    n_blocks = seq // MOBA_BLOCK
    return pl.pallas_call(
        functools.partial(_moba_select_kernel, n_blocks=n_blocks),
        grid=(batch, nq),
        in_specs=[pl.BlockSpec((t, MOBA_HEADS * HEAD_DIM), lambda b, i: (b * nq + i, 0)),
                  pl.BlockSpec((1, MOBA_HEADS // 2, LANES, LANES), lambda b, i: (b, 0, 0, 0))],
        out_specs=pl.BlockSpec((t, MOBA_HEADS * HEAD_DIM), lambda b, i: (b * nq + i, 0)),
        out_shape=jax.ShapeDtypeStruct(mqf.shape, jnp.float32),
        compiler_params=_compiler_params(2),
        name="moba_select",
    )(mqf, km2)


def _fill_bias_table(tab_ref, rel_ref, head, t):
    far = rel_ref[head, REL_BUCKETS - 1]
    r = lax.broadcasted_iota(jnp.int32, (t, 2 * t), 0)
    c = lax.broadcasted_iota(jnp.int32, (t, 2 * t), 1)
    dist = r - c + t
    bias = jnp.full((t, 2 * t), (rel_ref[head, 0] - far) * LOG2_E, jnp.float32)
    for k in range(1, REL_BUCKETS):
        bias = jnp.where(dist >= REL_THRESHOLDS[k], (rel_ref[head, k] - far) * LOG2_E, bias)
    tab_ref[...] = jnp.where(dist >= 0, bias, MASK_NEG)


def _softmax_step(s, v_aug, m_ref, acc_ref):
    m_prev = m_ref[...]
    m_new = jnp.maximum(m_prev, jnp.max(s, axis=1, keepdims=True))
    alpha = jnp.exp2(m_prev - m_new)
    p = jnp.exp2(s - jnp.tile(m_new, (1, s.shape[1] // LANES)))
    pv = jnp.dot(p.astype(jnp.bfloat16), v_aug, preferred_element_type=jnp.float32)
    acc_ref[...] = jnp.tile(alpha, (1, 2)) * acc_ref[...] + pv
    m_ref[...] = m_new


def _far_tiles(n, last, logits_of, values_of, sa_ref, sb_ref, m_ref, acc_ref):
    def run(first, count):
        for u in range(count):
            cur, nxt = (sa_ref, sb_ref) if u % 2 == 0 else (sb_ref, sa_ref)
            nxt[...] = logits_of(jnp.minimum(first + u + 1, last))
            _softmax_step(cur[...], values_of(first + u), m_ref, acc_ref)

    groups = lax.shift_right_logical(n, FAR_UNROLL.bit_length() - 1)

    def group(i, carry):
        run(i * FAR_UNROLL, FAR_UNROLL)
        return carry

    lax.fori_loop(0, groups, group, 0)
    done = groups * FAR_UNROLL
    count = FAR_UNROLL // 2
    while count >= 2:
        @pl.when(((n - done) & count) != 0)
        def _(done=done, count=count):
            run(done, count)
        done = done + ((n - done) & count)
        count //= 2


def _init_softmax(m_ref, acc_ref):
    m_ref[...] = jnp.full(m_ref.shape, MASK_NEG, jnp.float32)
    acc_ref[...] = jnp.zeros(acc_ref.shape, jnp.float32)


def _fill_values_aug(vaug_ref, v_ref):
    vaug_ref[:, 0:LANES] = v_ref[...]
    vaug_ref[:, LANES:2 * LANES] = jnp.ones(v_ref.shape, vaug_ref.dtype)


def _qk(q2, k):
    return lax.dot_general(q2, k, (((1,), (1,)), ((), ())), preferred_element_type=jnp.float32)


def _split_halves(q):
    lane = lax.broadcasted_iota(jnp.int32, q.shape, 1)
    zero = jnp.zeros_like(q)
    return jnp.concatenate([jnp.where(lane < LANES // 2, q, zero),
                            jnp.where(lane < LANES // 2, zero, q)], axis=0)


def _diff_attn_kernel(rel_ref, q_ref, k_ref, v_ref, lam_ref, subg_ref, o_ref,
                      tab_ref, vaug_ref, sa_ref, sb_ref, sd_ref, m_ref, acc_ref):
    t = DIFF_TILE
    n_q = q_ref.shape[0] // t
    head = pl.program_id(1)
    _fill_bias_table(tab_ref, rel_ref, head, t)
    _fill_values_aug(vaug_ref, v_ref)
    lam_p = lam_ref[...]
    lam = (jnp.exp(jnp.sum(lam_p[0:1] * lam_p[1:2], axis=1, keepdims=True))
           - jnp.exp(jnp.sum(lam_p[2:3] * lam_p[3:4], axis=1, keepdims=True)) + LAM_INIT)

    def rows_of(i):
        return pl.ds(pl.multiple_of(i * t, t), t)

    def queries_of(i):
        return _split_halves(q_ref[rows_of(i), :])

    def values_of(j):
        return vaug_ref[rows_of(j), :]

    def with_bias(s_ref, col0):
        return (s_ref[...].reshape(2, t, t) + tab_ref[:, col0:col0 + t][None]).reshape(2 * t, t)

    def finish(qi):
        o = acc_ref[:, 0:LANES] / acc_ref[:, LANES:2 * LANES]
        o = o[:t] - lam * o[t:]
        o = o * lax.rsqrt(jnp.mean(o * o, axis=1, keepdims=True) + LN_EPS) * subg_ref[...]
        o_ref[rows_of(qi), :] = (o * (1.0 - LAM_INIT)).astype(o_ref.dtype)

    def first_logits_of_next(qi):
        nxt = jnp.minimum(qi + 1, n_q - 1)
        sa_ref[...] = _qk(queries_of(nxt), k_ref[rows_of(0), :])

    q2 = queries_of(0)
    sd_ref[0:t, :] = _qk(q2[0:t], k_ref[rows_of(0), :])
    sd_ref[t:2 * t, :] = _qk(q2[t:2 * t], k_ref[rows_of(0), :])
    _init_softmax(m_ref, acc_ref)
    first_logits_of_next(0)
    _softmax_step(with_bias(sd_ref, t), values_of(0), m_ref, acc_ref)
    finish(0)

    def query_tile(qi, carry):
        q2 = queries_of(qi)

        def logits_of(j):
            return _qk(q2, k_ref[rows_of(j), :])

        def last_two(cur_ref, other_ref):
            other_ref[...] = logits_of(qi)
            _softmax_step(with_bias(cur_ref, 0), values_of(qi - 1), m_ref, acc_ref)
            first_logits_of_next(qi)
            _softmax_step(with_bias(other_ref, t), values_of(qi), m_ref, acc_ref)
            finish(qi)

        n_far = qi - 1
        _init_softmax(m_ref, acc_ref)
        _far_tiles(n_far, n_far, logits_of, values_of, sa_ref, sb_ref, m_ref, acc_ref)

        @pl.when((n_far & 1) == 0)
        def _():
            last_two(sa_ref, sb_ref)

        @pl.when((n_far & 1) == 1)
        def _():
            sb_ref[...] = logits_of(n_far)
            _softmax_step(sa_ref[...], values_of(n_far - 1), m_ref, acc_ref)
            last_two(sb_ref, sd_ref)

        return carry

    lax.fori_loop(1, n_q, query_tile, 0)


def _diff_attn(rel_t, qkv, lam_p, sub_g, batch, seq):
    t = DIFF_TILE
    once = pl.Buffered(1)
    grid_spec = pltpu.PrefetchScalarGridSpec(
        num_scalar_prefetch=1,
        grid=(batch, DIFF_HEADS),
        in_specs=[
            pl.BlockSpec((seq, LANES), lambda b, h, rel: (b, COL_DQ // LANES + h)),
            pl.BlockSpec((seq, LANES), lambda b, h, rel: (b, COL_DK // LANES + h)),
            pl.BlockSpec((seq, LANES), lambda b, h, rel: (b, COL_DV // LANES + h),
                         pipeline_mode=once),
            pl.BlockSpec((4, HEAD_DIM), lambda b, h, rel: (0, 0)),
            pl.BlockSpec((1, LANES), lambda b, h, rel: (0, 0)),
        ],
        out_specs=pl.BlockSpec((seq, LANES), lambda b, h, rel: (b, h)),
        scratch_shapes=[
            pltpu.VMEM((t, 2 * t), jnp.float32),
            pltpu.VMEM((seq, 2 * LANES), jnp.bfloat16),
            pltpu.VMEM((2 * t, t), jnp.float32),
            pltpu.VMEM((2 * t, t), jnp.float32),
            pltpu.VMEM((2 * t, t), jnp.float32),
            pltpu.VMEM((2 * t, LANES), jnp.float32),
            pltpu.VMEM((2 * t, 2 * LANES), jnp.float32),
        ],
    )
    return pl.pallas_call(
        _diff_attn_kernel,
        grid_spec=grid_spec,
        out_shape=jax.ShapeDtypeStruct((batch * seq, DIFF_HEADS * LANES), jnp.bfloat16),
        compiler_params=_compiler_params(2),
        name="diff_attn",
    )(rel_t, qkv, qkv, qkv, lam_p, sub_g)


def _moba_attn_kernel(rel_ref, q_ref, k_ref, v_ref, neg_ref, o_ref,
                      tab_ref, kaug_ref, vaug_ref, lhs_ref, sa_ref, sb_ref, m_ref, acc_ref):
    t = q_ref.shape[0]
    seq = k_ref.shape[0]
    half = LANES // 2
    blocks_per_tile = t // MOBA_BLOCK
    pair = pl.program_id(1)
    qi = pl.program_id(2)

    @pl.when(qi == 0)
    def _():
        _fill_bias_table(tab_ref.at[0], rel_ref, DIFF_HEADS + 2 * pair, t)
        _fill_bias_table(tab_ref.at[1], rel_ref, DIFF_HEADS + 2 * pair + 1, t)
        _fill_values_aug(vaug_ref, v_ref)
        kaug_ref[:, 0:LANES] = k_ref[...]
        key_blk = lax.broadcasted_iota(jnp.int32, (seq, LANES), 0) // MOBA_BLOCK
        lane_k = lax.broadcasted_iota(jnp.int32, (seq, LANES), 1)
        kaug_ref[:, LANES:2 * LANES] = jnp.where(key_blk == lane_k, 1.0, 0.0).astype(kaug_ref.dtype)

    q2 = _split_halves(q_ref[...])
    neg = neg_ref[...]
    neg2 = jnp.concatenate([neg, pltpu.roll(neg, half, axis=1)], axis=0)
    lhs_ref[:, 0:LANES] = q2
    lhs_ref[:, LANES:2 * LANES] = neg2.astype(lhs_ref.dtype)

    def keys_of(j):
        return kaug_ref[pl.ds(pl.multiple_of(j * t, t), t), :]

    def logits_of(j):
        return _qk(lhs_ref[...], keys_of(j))

    def values_of(j):
        return vaug_ref[pl.ds(pl.multiple_of(j * t, t), t), :]

    prev = jnp.maximum(qi - 1, 0)
    _init_softmax(m_ref, acc_ref)
    k_own = keys_of(qi)[:, 0:LANES]
    s = jnp.stack([_qk(q2[0:t], k_own), _qk(q2[t:2 * t], k_own)]) + tab_ref[:, :, t:2 * t]
    row_blk = lax.broadcasted_iota(jnp.int32, (2, t, t), 1) // MOBA_BLOCK
    col_blk = lax.broadcasted_iota(jnp.int32, (2, t, t), 2) // MOBA_BLOCK
    lane = lax.broadcasted_iota(jnp.int32, (2 * t, LANES), 1)
    for cb in range(blocks_per_tile - 1):
        chosen = jnp.sum(jnp.where(lane == qi * blocks_per_tile + cb, neg2, 0.0),
                         axis=1, keepdims=True).reshape(2, t, 1)
        s = s + jnp.where((col_blk == cb) & (row_blk > cb), chosen, 0.0)
    sa_ref[...] = s.reshape(2 * t, t)
    sb_ref[...] = (logits_of(prev).reshape(2, t, t) + tab_ref[:, :, 0:t]).reshape(2 * t, t)
    _softmax_step(sa_ref[...], values_of(qi), m_ref, acc_ref)

    @pl.when(qi >= 1)
    def _():
        sa_ref[...] = logits_of(0)
        _softmax_step(sb_ref[...], values_of(prev), m_ref, acc_ref)

    n_far = jnp.maximum(qi - 1, 0)
    _far_tiles(n_far, n_far - 1, logits_of, values_of, sa_ref, sb_ref, m_ref, acc_ref)

    @pl.when((n_far & 1) == 1)
    def _():
        _softmax_step(sa_ref[...], values_of(n_far - 1), m_ref, acc_ref)

    o = acc_ref[:, 0:LANES] / acc_ref[:, LANES:2 * LANES]
    lane_o = lax.broadcasted_iota(jnp.int32, (t, LANES), 1)
    o_ref[...] = jnp.where(lane_o < half, o[:t], o[t:]).astype(o_ref.dtype)


def _moba_attn(rel_t, qkv, neg, batch, seq):
    t = MOBA_TILE
    nq = seq // t
    n_pairs = MOBA_HEADS // 2
    grid_spec = pltpu.PrefetchScalarGridSpec(
        num_scalar_prefetch=1,
        grid=(batch, n_pairs, nq),
        in_specs=[
            pl.BlockSpec((t, LANES), lambda b, p, i, rel: (b * nq + i, COL_MQ // LANES + p)),
            pl.BlockSpec((seq, LANES), lambda b, p, i, rel: (b, COL_MK // LANES + p)),
            pl.BlockSpec((seq, LANES), lambda b, p, i, rel: (b, COL_MV // LANES + p)),
            pl.BlockSpec((t, LANES), lambda b, p, i, rel: (b * nq + i, p)),
        ],
        out_specs=pl.BlockSpec((t, LANES), lambda b, p, i, rel: (b * nq + i, p)),
        scratch_shapes=[
            pltpu.VMEM((2, t, 2 * t), jnp.float32),
            pltpu.VMEM((seq, 2 * LANES), jnp.bfloat16),
            pltpu.VMEM((seq, 2 * LANES), jnp.bfloat16),
            pltpu.VMEM((2 * t, 2 * LANES), jnp.bfloat16),
            pltpu.VMEM((2 * t, t), jnp.float32),
            pltpu.VMEM((2 * t, t), jnp.float32),
            pltpu.VMEM((2 * t, LANES), jnp.float32),
            pltpu.VMEM((2 * t, 2 * LANES), jnp.float32),
        ],
    )
    return pl.pallas_call(
        _moba_attn_kernel,
        grid_spec=grid_spec,
        out_shape=jax.ShapeDtypeStruct((batch * seq, MOBA_HEADS * HEAD_DIM), jnp.bfloat16),
        compiler_params=_compiler_params(3),
        name="moba_attn",
    )(rel_t, qkv, qkv, qkv, neg)


def _merge_kernel(ad_ref, am_ref, gl_ref, bg_ref, h_ref, wbd_ref, wbm_ref, wo_ref,
                  g_ref, b_ref, o_ref):
    for rows in _row_chains(o_ref.shape[0]):
        y_d = jnp.dot(ad_ref[rows, :], wbd_ref[...], preferred_element_type=jnp.float32)
        y_m = jnp.dot(am_ref[rows, :], wbm_ref[...], preferred_element_type=jnp.float32)
        gates = jax.nn.sigmoid(gl_ref[rows, :] + bg_ref[...])
        mixed = gates[:, :D_MODEL] * y_d + gates[:, D_MODEL:] * y_m
        out = jnp.dot(mixed.astype(jnp.bfloat16), wo_ref[...], preferred_element_type=jnp.float32)
        o_ref[rows, :] = _layer_norm(DEEPNORM_ALPHA * h_ref[rows, :] + out, g_ref[...], b_ref[...])


def _merge(attn_d, attn_m, gl, b_gate, h, wbd, wbm, wout, ln_g, ln_b):
    n_rows = h.shape[0]
    tm = ROW_TILE
    row = lambda cols: pl.BlockSpec((tm, cols), lambda i: (i, 0))
    return pl.pallas_call(
        _merge_kernel,
        grid=(n_rows // tm,),
        in_specs=[row(D_MODEL), row(MOBA_HEADS * HEAD_DIM), row(2 * D_MODEL),
                  _resident((1, 2 * D_MODEL)), row(D_MODEL),
                  _resident(wbd.shape), _resident(wbm.shape), _resident(wout.shape),
                  _resident((1, D_MODEL)), _resident((1, D_MODEL))],
        out_specs=row(D_MODEL),
        out_shape=jax.ShapeDtypeStruct((n_rows, D_MODEL), jnp.float32),
        compiler_params=_compiler_params(1),
        name="merge",
    )(attn_d, attn_m, gl, b_gate, h, wbd, wbm, wout, ln_g, ln_b)


def _xattn_kv_kernel(mem_ref, wk_ref, wv_ref, k_ref, v_ref):
    mb = mem_ref[...].astype(jnp.bfloat16)
    k_ref[...] = jnp.dot(mb, wk_ref[...], preferred_element_type=jnp.float32).astype(k_ref.dtype)
    v_ref[...] = jnp.dot(mb, wv_ref[...], preferred_element_type=jnp.float32).astype(v_ref.dtype)


def _xattn_kv(mem2, wk, wv, batch, n_mem):
    blk = pl.BlockSpec((n_mem, D_MODEL), lambda b: (b, 0))
    return pl.pallas_call(
        _xattn_kv_kernel,
        grid=(batch,),
        in_specs=[blk, _resident(wk.shape), _resident(wv.shape)],
        out_specs=[blk, blk],
        out_shape=[jax.ShapeDtypeStruct(mem2.shape, jnp.bfloat16)] * 2,
        compiler_params=_compiler_params(1),
        name="xattn_kv",
    )(mem2, wk, wv)


def _xattn_kernel(h_ref, wq_ref, k_ref, v_ref, wo_ref, g_ref, b_ref, o_ref):
    for rows in _row_chains(o_ref.shape[0]):
        h = h_ref[rows, :]
        q = jnp.dot(h.astype(jnp.bfloat16), wq_ref[...], preferred_element_type=jnp.float32)
        q = (q * XATTN_HEAD_DIM ** -0.5).astype(jnp.bfloat16)
        outs = []
        for hd in range(XATTN_HEADS):
            cols = slice(hd * XATTN_HEAD_DIM, (hd + 1) * XATTN_HEAD_DIM)
            s = _qk(q[:, cols], k_ref[:, cols])
            p = jnp.exp(s - jnp.max(s, axis=1, keepdims=True))
            p = p / jnp.sum(p, axis=1, keepdims=True)
            outs.append(jnp.dot(p.astype(jnp.bfloat16), v_ref[:, cols],
                                preferred_element_type=jnp.float32))
        o = jnp.concatenate(outs, axis=1).astype(jnp.bfloat16)
        xa = jnp.dot(o, wo_ref[...], preferred_element_type=jnp.float32)
        o_ref[rows, :] = _layer_norm(DEEPNORM_ALPHA * h + xa, g_ref[...], b_ref[...])


def _xattn(h1, wq, kx, vx, wo, ln_g, ln_b, batch, seq, n_mem):
    tm = ROW_TILE
    nq = seq // tm
    row = pl.BlockSpec((tm, D_MODEL), lambda b, i: (b * nq + i, 0))
    kv = pl.BlockSpec((n_mem, D_MODEL), lambda b, i: (b, 0))
    return pl.pallas_call(
        _xattn_kernel,
        grid=(batch, nq),
        in_specs=[row, _resident(wq.shape), kv, kv, _resident(wo.shape),
                  _resident((1, D_MODEL)), _resident((1, D_MODEL))],
        out_specs=row,
        out_shape=jax.ShapeDtypeStruct(h1.shape, jnp.float32),
        compiler_params=_compiler_params(2),
        name="xattn",
    )(h1, wq, kx, vx, wo, ln_g, ln_b)


def _mlp_kernel(h_ref, w1_ref, w2_ref, g_ref, b_ref, o_ref, hb_ref, acc_ref):
    j = pl.program_id(1)

    @pl.when(j == 0)
    def _():
        hb_ref[...] = h_ref[...].astype(jnp.bfloat16)
        acc_ref[...] = jnp.zeros(acc_ref.shape, jnp.float32)

    for rows in _row_chains(o_ref.shape[0]):
        u = jnp.dot(hb_ref[rows, :], w1_ref[...], preferred_element_type=jnp.float32)
        u = jnp.square(jnp.maximum(u, 0.0)).astype(jnp.bfloat16)
        acc_ref[rows, :] += jnp.dot(u, w2_ref[...], preferred_element_type=jnp.float32)

    @pl.when(j == pl.num_programs(1) - 1)
    def _():
        o_ref[...] = _layer_norm(DEEPNORM_ALPHA * h_ref[...] + acc_ref[...],
                                 g_ref[...], b_ref[...])


def _mlp(h2, w1, w2, ln_g, ln_b):
    n_rows = h2.shape[0]
    tm = MLP_ROW_TILE
    row = pl.BlockSpec((tm, D_MODEL), lambda i, j: (i, 0))
    return pl.pallas_call(
        _mlp_kernel,
        grid=(n_rows // tm, D_FF // FF_TILE),
        in_specs=[row,
                  pl.BlockSpec((D_MODEL, FF_TILE), lambda i, j: (0, j)),
                  pl.BlockSpec((FF_TILE, D_MODEL), lambda i, j: (j, 0)),
                  _resident((1, D_MODEL)), _resident((1, D_MODEL))],
        out_specs=row,
        out_shape=jax.ShapeDtypeStruct(h2.shape, jnp.float32),
        scratch_shapes=[pltpu.VMEM((tm, D_MODEL), jnp.bfloat16),
                        pltpu.VMEM((tm, D_MODEL), jnp.float32)],
        compiler_params=_compiler_params(2),
        name="mlp",
    )(h2, w1, w2, ln_g, ln_b)


def kernel(x, mem, ln_in_g, ln_in_b, rel_table, w_in, b_gate, lam_q1, lam_k1, lam_q2, lam_k2,
           diff_sub_g, w_br_diff, w_br_moba, w_out, ln1_g, ln1_b, wq_x, wk_x, wv_x, wo_x,
           ln2_g, ln2_b, w_ff1, w_ff2, ln3_g, ln3_b):
    batch, seq, d = x.shape
    n_mem = mem.shape[1]
    assert d == D_MODEL and w_in.shape == (DEPTH, D_MODEL, W_IN_COLS)
    assert seq % ROW_TILE == 0 and seq % DIFF_TILE == 0 and seq % SEL_TILE == 0
    assert (batch * seq) % MLP_ROW_TILE == 0
    assert seq % MOBA_TILE == 0 and MOBA_TILE % MOBA_BLOCK == 0
    assert seq // MOBA_BLOCK <= LANES // 2 and MOBA_TILE >= REL_FAR_DIST <= DIFF_TILE
    bf = lambda w: w.astype(jnp.bfloat16)
    vec = lambda v: v.reshape(1, -1).astype(jnp.float32)
    l = 0

    x2 = x.reshape(batch * seq, d)
    h, qkv, mqf, gl, kmean = _ln_inproj(x2, vec(ln_in_g), vec(ln_in_b), bf(w_in[l]))

    n_blocks = seq // MOBA_BLOCK
    km = kmean.reshape(batch, n_blocks, MOBA_HEADS // 2, 2, HEAD_DIM)
    km = jnp.pad(km, ((0, 0), (0, LANES // 2 - n_blocks), (0, 0), (0, 0), (0, 0)))
    eye2 = jnp.eye(2, dtype=jnp.float32)
    km2 = jnp.einsum('bnpad,ac->bpcnad', km, eye2).reshape(batch, MOBA_HEADS // 2, LANES, LANES)
    neg = _moba_select(mqf, km2, batch, seq)

    rel_t = rel_table.T.astype(jnp.float32)
    lam_p = jnp.concatenate([lam_q1[l:l + 1], lam_k1[l:l + 1], lam_q2[l:l + 1], lam_k2[l:l + 1]],
                            axis=0).astype(jnp.float32)
    attn_d = _diff_attn(rel_t, qkv, lam_p, vec(diff_sub_g[l]), batch, seq)
    attn_m = _moba_attn(rel_t, qkv, neg, batch, seq)

    h1 = _merge(attn_d, attn_m, gl, vec(b_gate[l]), h, bf(w_br_diff[l]), bf(w_br_moba[l]),
                bf(w_out[l]), vec(ln1_g[l]), vec(ln1_b[l]))

    kx, vx = _xattn_kv(mem.reshape(batch * n_mem, d), bf(wk_x[l]), bf(wv_x[l]), batch, n_mem)
    h2 = _xattn(h1, bf(wq_x[l]), kx, vx, bf(wo_x[l]), vec(ln2_g[l]), vec(ln2_b[l]),
                batch, seq, n_mem)

    h3 = _mlp(h2, bf(w_ff1[l]), bf(w_ff2[l]), vec(ln3_g[l]), vec(ln3_b[l]))
    return h3.reshape(batch, seq, d)
```

```python
import functools
import math

import numpy as np
import jax
import jax.numpy as jnp
from jax import lax
from jax.experimental import pallas as pl
from jax.experimental.pallas import tpu as pltpu

D_MODEL = 1024
HEAD_DIM = 64
DIFF_HEADS = 8
MOBA_HEADS = 8
MOBA_BLOCK = 256
MOBA_TOPK = 3
XATTN_HEADS = 4
XATTN_HEAD_DIM = D_MODEL // XATTN_HEADS
D_FF = 4 * D_MODEL
REL_BUCKETS = 32
REL_MAX_DIST = 128
LN_EPS = 1e-5
DEPTH = 1
DEEPNORM_ALPHA = (2.0 * DEPTH) ** 0.25
LAM_INIT = 0.8 - 0.6 * math.exp(-0.3 * 0)
LOG2_E = math.log2(math.e)

COL_DQ = 0
COL_DK = 1024
COL_DV = 2048
COL_MQ = 3072
COL_MK = 3584
COL_MV = 4096
COL_GATE = 4608
N_QKV_COLS = 4608
W_IN_COLS = 6656

LANES = 128
VMEM_LIMIT_BYTES = 56 * 1024 * 1024
MOBA_VMEM_LIMIT_BYTES = 62 * 1024 * 1024

ROW_TILE = 512
DIFF_TILE = 512
MOBA_TILE = 512
SEL_TILE = 512
FF_TILE = 1024
PROJ_CHUNK = 512
FAR_UNROLL = 16
ROW_CHAINS = 2
MLP_ROW_TILE = 1024

MASK_NEG = -1e30


def _rel_thresholds():
    max_exact = REL_BUCKETS // 2
    n = np.arange(0, 4 * REL_MAX_DIST)
    nf = np.maximum(n, 1).astype(np.float32)
    large = max_exact + (np.log(nf / max_exact) / math.log(REL_MAX_DIST / max_exact)
                         * (REL_BUCKETS - max_exact)).astype(np.int32)
    large = np.minimum(large, REL_BUCKETS - 1)
    bucket = np.where(n < max_exact, n, large)
    thr = [int(np.argmax(bucket >= k)) for k in range(REL_BUCKETS)]
    assert all(bucket[t] == k for k, t in enumerate(thr))
    return thr


REL_THRESHOLDS = _rel_thresholds()
REL_FAR_DIST = REL_THRESHOLDS[-1]


def _layer_norm(z, g, b):
    mu = jnp.mean(z, axis=-1, keepdims=True)
    zc = z - mu
    var = jnp.mean(zc * zc, axis=-1, keepdims=True)
    return zc * lax.rsqrt(var + LN_EPS) * g + b


def _compiler_params(n_grid, vmem_limit_bytes=VMEM_LIMIT_BYTES):
    return pltpu.CompilerParams(
        dimension_semantics=("arbitrary",) * n_grid,
        vmem_limit_bytes=vmem_limit_bytes,
    )


def _row_chains(n_rows, chains=ROW_CHAINS):
    step = n_rows // chains
    return [pl.ds(r0, step) for r0 in range(0, n_rows, step)]


def _resident(shape):
    nd = len(shape)
    return pl.BlockSpec(shape, lambda *_: (0,) * nd, pipeline_mode=pl.Buffered(1))


def _ln_inproj_kernel(x_ref, g_ref, b_ref, w_ref, h_ref, qkv_ref, mqf_ref, gl_ref, kmean_ref):
    h = _layer_norm(x_ref[...], g_ref[...], b_ref[...])
    h_ref[...] = h
    hb = h.astype(jnp.bfloat16)
    q_scale = HEAD_DIM ** -0.5 * LOG2_E
    for c0 in range(0, W_IN_COLS, PROJ_CHUNK):
        acc = jnp.dot(hb, w_ref[:, c0:c0 + PROJ_CHUNK], preferred_element_type=jnp.float32)
        if c0 < COL_GATE:
            is_q = c0 < COL_DK or COL_MQ <= c0 < COL_MK
            if COL_MQ <= c0 < COL_MK:
                mqf_ref[:, c0 - COL_MQ:c0 - COL_MQ + PROJ_CHUNK] = acc
            if COL_MK <= c0 < COL_MV:
                nblk = acc.shape[0] // MOBA_BLOCK
                km = jnp.mean(acc.reshape(nblk, MOBA_BLOCK, PROJ_CHUNK), axis=1)
                kmean_ref[0, :, c0 - COL_MK:c0 - COL_MK + PROJ_CHUNK] = km
            val = acc * q_scale if is_q else acc
            qkv_ref[:, c0:c0 + PROJ_CHUNK] = val.astype(jnp.bfloat16)
        else:
            gl_ref[:, c0 - COL_GATE:c0 - COL_GATE + PROJ_CHUNK] = acc


def _ln_inproj(x2, ln_g, ln_b, w_in_bf):
    n_rows = x2.shape[0]
    tm = ROW_TILE
    nblk = tm // MOBA_BLOCK
    grid = (n_rows // tm,)
    row = lambda cols: pl.BlockSpec((tm, cols), lambda i: (i, 0))
    return pl.pallas_call(
        _ln_inproj_kernel,
        grid=grid,
        in_specs=[row(D_MODEL), _resident((1, D_MODEL)), _resident((1, D_MODEL)),
                  _resident((D_MODEL, W_IN_COLS))],
        out_specs=[row(D_MODEL), row(N_QKV_COLS), row(MOBA_HEADS * HEAD_DIM), row(2 * D_MODEL),
                   pl.BlockSpec((1, nblk, MOBA_HEADS * HEAD_DIM), lambda i: (i, 0, 0))],
        out_shape=[
            jax.ShapeDtypeStruct((n_rows, D_MODEL), jnp.float32),
            jax.ShapeDtypeStruct((n_rows, N_QKV_COLS), jnp.bfloat16),
            jax.ShapeDtypeStruct((n_rows, MOBA_HEADS * HEAD_DIM), jnp.float32),
            jax.ShapeDtypeStruct((n_rows, 2 * D_MODEL), jnp.float32),
            jax.ShapeDtypeStruct((n_rows // tm, nblk, MOBA_HEADS * HEAD_DIM), jnp.float32),
        ],
        compiler_params=_compiler_params(1),
        name="ln_inproj",
    )(x2, ln_g, ln_b, w_in_bf)


def _moba_select_kernel(q_ref, km_ref, neg_ref, *, n_blocks):
    t = q_ref.shape[0]
    half = LANES // 2
    row0 = pl.program_id(1) * t
    shape3 = (2, half, t)
    blk = lax.broadcasted_iota(jnp.int32, shape3, 1)
    cur = (row0 + lax.broadcasted_iota(jnp.int32, shape3, 2)) // MOBA_BLOCK
    blk_f = blk.astype(jnp.float32)
    ninf = jnp.float32(-jnp.inf)
    for pair in range(MOBA_HEADS // 2):
        q = q_ref[:, pair * LANES:(pair + 1) * LANES]
        gate = lax.dot_general(km_ref[0, pair], q, (((1,), (1,)), ((), ())),
                               preferred_element_type=jnp.float32,
                               precision=lax.Precision.HIGHEST)
        g = jnp.where((blk < cur) & (blk < n_blocks), gate.reshape(shape3), ninf)
        neg = jnp.full(shape3, MASK_NEG, jnp.float32)
        for rank in range(MOBA_TOPK):
            mx = jnp.max(g, axis=1, keepdims=True)
            ix = jnp.min(jnp.where(g == mx, blk_f, float(half)), axis=1, keepdims=True)
            pick = blk_f == ix
            neg = jnp.where(pick & (rank < cur), 0.0, neg)
            g = jnp.where(pick, ninf, g)
        neg_ref[:, pair * LANES:(pair + 1) * LANES] = neg.reshape(LANES, t).T.astype(neg_ref.dtype)


def _moba_select(mqf, km2, batch, seq):
    t = SEL_TILE
    nq = seq // t
    n_blocks = seq // MOBA_BLOCK
    return pl.pallas_call(
        functools.partial(_moba_select_kernel, n_blocks=n_blocks),
        grid=(batch, nq),
        in_specs=[pl.BlockSpec((t, MOBA_HEADS * HEAD_DIM), lambda b, i: (b * nq + i, 0)),
                  pl.BlockSpec((1, MOBA_HEADS // 2, LANES, LANES), lambda b, i: (b, 0, 0, 0))],
        out_specs=pl.BlockSpec((t, MOBA_HEADS * HEAD_DIM), lambda b, i: (b * nq + i, 0)),
        out_shape=jax.ShapeDtypeStruct(mqf.shape, jnp.bfloat16),
        compiler_params=_compiler_params(2),
        name="moba_select",
    )(mqf, km2)


def _fill_bias_table(tab_ref, rel_ref, head, t):
    far = rel_ref[head, REL_BUCKETS - 1]
    r = lax.broadcasted_iota(jnp.int32, (t, 2 * t), 0)
    c = lax.broadcasted_iota(jnp.int32, (t, 2 * t), 1)
    dist = r - c + t
    bias = jnp.full((t, 2 * t), (rel_ref[head, 0] - far) * LOG2_E, jnp.float32)
    for k in range(1, REL_BUCKETS):
        bias = jnp.where(dist >= REL_THRESHOLDS[k], (rel_ref[head, k] - far) * LOG2_E, bias)
    tab_ref[...] = jnp.where(dist >= 0, bias, MASK_NEG)


def _softmax_step(s, v_aug, m_ref, acc_ref):
    m_prev = m_ref[...]
    m_new = jnp.maximum(m_prev, jnp.max(s, axis=1, keepdims=True))
    alpha = jnp.exp2(m_prev - m_new)
    p = jnp.exp2(s - jnp.tile(m_new, (1, s.shape[1] // LANES)))
    pv = jnp.dot(p.astype(jnp.bfloat16), v_aug, preferred_element_type=jnp.float32)
    acc_ref[...] = jnp.tile(alpha, (1, 2)) * acc_ref[...] + pv
    m_ref[...] = m_new


def _far_tiles(n, last, logits_of, values_of, sa_ref, sb_ref, m_ref, acc_ref):
    def run(first, count):
        for u in range(count):
            cur, nxt = (sa_ref, sb_ref) if u % 2 == 0 else (sb_ref, sa_ref)
            nxt[...] = logits_of(jnp.minimum(first + u + 1, last))
            _softmax_step(cur[...], values_of(first + u), m_ref, acc_ref)

    groups = lax.shift_right_logical(n, FAR_UNROLL.bit_length() - 1)

    def group(i, carry):
        run(i * FAR_UNROLL, FAR_UNROLL)
        return carry

    lax.fori_loop(0, groups, group, 0)
    done = groups * FAR_UNROLL
    count = FAR_UNROLL // 2
    while count >= 2:
        @pl.when(((n - done) & count) != 0)
        def _(done=done, count=count):
            run(done, count)
        done = done + ((n - done) & count)
        count //= 2


def _init_softmax(m_ref, acc_ref):
    m_ref[...] = jnp.full(m_ref.shape, MASK_NEG, jnp.float32)
    acc_ref[...] = jnp.zeros(acc_ref.shape, jnp.float32)


def _fill_values_aug(vaug_ref, v_ref):
    vaug_ref[:, 0:LANES] = v_ref[...]
    vaug_ref[:, LANES:2 * LANES] = jnp.ones(v_ref.shape, vaug_ref.dtype)


def _qk(q2, k):
    return lax.dot_general(q2, k, (((1,), (1,)), ((), ())), preferred_element_type=jnp.float32)


def _split_halves(q):
    lane = lax.broadcasted_iota(jnp.int32, q.shape, 1)
    zero = jnp.zeros_like(q)
    return jnp.concatenate([jnp.where(lane < LANES // 2, q, zero),
                            jnp.where(lane < LANES // 2, zero, q)], axis=0)


def _diff_attn_kernel(rel_ref, q_ref, k_ref, v_ref, lam_ref, subg_ref, o_ref,
                      tab_ref, vaug_ref, sa_ref, sb_ref, sd_ref, m_ref, acc_ref):
    t = DIFF_TILE
    n_q = q_ref.shape[0] // t
    head = pl.program_id(1)
    _fill_bias_table(tab_ref, rel_ref, head, t)
    _fill_values_aug(vaug_ref, v_ref)
    lam_p = lam_ref[...]
    lam = (jnp.exp(jnp.sum(lam_p[0:1] * lam_p[1:2], axis=1, keepdims=True))
           - jnp.exp(jnp.sum(lam_p[2:3] * lam_p[3:4], axis=1, keepdims=True)) + LAM_INIT)

    def rows_of(i):
        return pl.ds(pl.multiple_of(i * t, t), t)

    def queries_of(i):
        return _split_halves(q_ref[rows_of(i), :])

    def values_of(j):
        return vaug_ref[rows_of(j), :]

    def with_bias(s_ref, col0):
        return (s_ref[...].reshape(2, t, t) + tab_ref[:, col0:col0 + t][None]).reshape(2 * t, t)

    def finish(qi):
        o = acc_ref[:, 0:LANES] / acc_ref[:, LANES:2 * LANES]
        o = o[:t] - lam * o[t:]
        o = o * lax.rsqrt(jnp.mean(o * o, axis=1, keepdims=True) + LN_EPS) * subg_ref[...]
        o_ref[rows_of(qi), :] = (o * (1.0 - LAM_INIT)).astype(o_ref.dtype)

    def first_logits_of_next(qi):
        nxt = jnp.minimum(qi + 1, n_q - 1)
        sa_ref[...] = _qk(queries_of(nxt), k_ref[rows_of(0), :])

    q2 = queries_of(0)
    sd_ref[0:t, :] = _qk(q2[0:t], k_ref[rows_of(0), :])
    sd_ref[t:2 * t, :] = _qk(q2[t:2 * t], k_ref[rows_of(0), :])
    _init_softmax(m_ref, acc_ref)
    first_logits_of_next(0)
    _softmax_step(with_bias(sd_ref, t), values_of(0), m_ref, acc_ref)
    finish(0)

    def query_tile(qi, carry):
        q2 = queries_of(qi)

        def logits_of(j):
            return _qk(q2, k_ref[rows_of(j), :])

        def last_two(cur_ref, other_ref):
            other_ref[...] = logits_of(qi)
            _softmax_step(with_bias(cur_ref, 0), values_of(qi - 1), m_ref, acc_ref)
            first_logits_of_next(qi)
            _softmax_step(with_bias(other_ref, t), values_of(qi), m_ref, acc_ref)
            finish(qi)

        n_far = qi - 1
        _init_softmax(m_ref, acc_ref)
        _far_tiles(n_far, n_far, logits_of, values_of, sa_ref, sb_ref, m_ref, acc_ref)

        @pl.when((n_far & 1) == 0)
        def _():
            last_two(sa_ref, sb_ref)

        @pl.when((n_far & 1) == 1)
        def _():
            sb_ref[...] = logits_of(n_far)
            _softmax_step(sa_ref[...], values_of(n_far - 1), m_ref, acc_ref)
            last_two(sb_ref, sd_ref)

        return carry

    lax.fori_loop(1, n_q, query_tile, 0)


def _diff_attn(rel_t, qkv, lam_p, sub_g, batch, seq):
    t = DIFF_TILE
    once = pl.Buffered(1)
    grid_spec = pltpu.PrefetchScalarGridSpec(
        num_scalar_prefetch=1,
        grid=(batch, DIFF_HEADS),
        in_specs=[
            pl.BlockSpec((seq, LANES), lambda b, h, rel: (b, COL_DQ // LANES + h)),
            pl.BlockSpec((seq, LANES), lambda b, h, rel: (b, COL_DK // LANES + h)),
            pl.BlockSpec((seq, LANES), lambda b, h, rel: (b, COL_DV // LANES + h),
                         pipeline_mode=once),
            pl.BlockSpec((4, HEAD_DIM), lambda b, h, rel: (0, 0)),
            pl.BlockSpec((1, LANES), lambda b, h, rel: (0, 0)),
        ],
        out_specs=pl.BlockSpec((seq, LANES), lambda b, h, rel: (b, h)),
        scratch_shapes=[
            pltpu.VMEM((t, 2 * t), jnp.float32),
            pltpu.VMEM((seq, 2 * LANES), jnp.bfloat16),
            pltpu.VMEM((2 * t, t), jnp.float32),
            pltpu.VMEM((2 * t, t), jnp.float32),
            pltpu.VMEM((2 * t, t), jnp.float32),
            pltpu.VMEM((2 * t, LANES), jnp.float32),
            pltpu.VMEM((2 * t, 2 * LANES), jnp.float32),
        ],
    )
    return pl.pallas_call(
        _diff_attn_kernel,
        grid_spec=grid_spec,
        out_shape=jax.ShapeDtypeStruct((batch * seq, DIFF_HEADS * LANES), jnp.bfloat16),
        compiler_params=_compiler_params(2),
        name="diff_attn",
    )(rel_t, qkv, qkv, qkv, lam_p, sub_g)


def _moba_attn_kernel(rel_ref, q_ref, k_ref, v_ref, neg_ref, o_ref,
                      tab_ref, kaug_ref, vaug_ref, lhs_ref, sa_ref, sb_ref, sd_ref,
                      m_ref, acc_ref):
    t = MOBA_TILE
    seq = k_ref.shape[0]
    n_q = seq // t
    half = LANES // 2
    blocks_per_tile = t // MOBA_BLOCK
    pair = pl.program_id(1)

    _fill_bias_table(tab_ref.at[0], rel_ref, DIFF_HEADS + 2 * pair, t)
    _fill_bias_table(tab_ref.at[1], rel_ref, DIFF_HEADS + 2 * pair + 1, t)
    _fill_values_aug(vaug_ref, v_ref)
    kaug_ref[:, 0:LANES] = k_ref[...]
    key_blk = lax.broadcasted_iota(jnp.int32, (seq, LANES), 0) // MOBA_BLOCK
    lane_k = lax.broadcasted_iota(jnp.int32, (seq, LANES), 1)
    kaug_ref[:, LANES:2 * LANES] = jnp.where(key_blk == lane_k, 1.0, 0.0).astype(kaug_ref.dtype)

    def rows_of(i):
        return pl.ds(pl.multiple_of(i * t, t), t)

    def keys_of(j):
        return kaug_ref[rows_of(j), :]

    def values_of(j):
        return vaug_ref[rows_of(j), :]

    def masks_of(i):
        neg = neg_ref[rows_of(i), :].astype(jnp.float32)
        return jnp.concatenate([neg, pltpu.roll(neg, half, axis=1)], axis=0)

    def lhs_of(i):
        q2 = _split_halves(q_ref[rows_of(i), :])
        return jnp.concatenate([q2, masks_of(i).astype(q2.dtype)], axis=1)

    def with_bias(s_ref, col0):
        return (s_ref[...].reshape(2, t, t) + tab_ref[:, :, col0:col0 + t]).reshape(2 * t, t)

    def own_logits(i, s_ref):
        s = with_bias(s_ref, t).reshape(2, t, t)
        neg2 = masks_of(i)
        lane = lax.broadcasted_iota(jnp.int32, (2 * t, LANES), 1)
        cols = []
        for cb in range(blocks_per_tile):
            part = s[:, :, cb * MOBA_BLOCK:(cb + 1) * MOBA_BLOCK]
            if cb < blocks_per_tile - 1:
                chosen = jnp.sum(jnp.where(lane == i * blocks_per_tile + cb, neg2, 0.0),
                                 axis=1, keepdims=True).reshape(2, t, 1)
                later = (cb + 1) * MOBA_BLOCK
                part = jnp.concatenate([part[:, :later], part[:, later:] + chosen[:, later:]],
                                       axis=1)
            cols.append(part)
        return jnp.concatenate(cols, axis=2).reshape(2 * t, t)

    def finish(qi):
        o = acc_ref[:, 0:LANES] / acc_ref[:, LANES:2 * LANES]
        lane_o = lax.broadcasted_iota(jnp.int32, (t, LANES), 1)
        o_ref[rows_of(qi), :] = jnp.where(lane_o < half, o[:t], o[t:]).astype(o_ref.dtype)

    def first_logits_of_next(qi):
        sa_ref[...] = _qk(lhs_of(jnp.minimum(qi + 1, n_q - 1)), keys_of(0))

    q2 = _split_halves(q_ref[rows_of(0), :])
    k_own = keys_of(0)[:, 0:LANES]
    sd_ref[0:t, :] = _qk(q2[0:t], k_own)
    sd_ref[t:2 * t, :] = _qk(q2[t:2 * t], k_own)
    _init_softmax(m_ref, acc_ref)
    first_logits_of_next(0)
    _softmax_step(own_logits(0, sd_ref), values_of(0), m_ref, acc_ref)
    finish(0)

    def query_tile(qi, carry):
        lhs_ref[...] = lhs_of(qi)

        def logits_of(j):
            return _qk(lhs_ref[...], keys_of(j))

        def last_two(cur_ref, other_ref):
            other_ref[...] = _qk(lhs_ref[:, 0:LANES], keys_of(qi)[:, 0:LANES])
            _softmax_step(with_bias(cur_ref, 0), values_of(qi - 1), m_ref, acc_ref)
            first_logits_of_next(qi)
            _softmax_step(own_logits(qi, other_ref), values_of(qi), m_ref, acc_ref)
            finish(qi)

        n_far = qi - 1
        _init_softmax(m_ref, acc_ref)
        _far_tiles(n_far, n_far, logits_of, values_of, sa_ref, sb_ref, m_ref, acc_ref)

        @pl.when((n_far & 1) == 0)
        def _():
            last_two(sa_ref, sb_ref)

        @pl.when((n_far & 1) == 1)
        def _():
            sb_ref[...] = logits_of(n_far)
            _softmax_step(sa_ref[...], values_of(n_far - 1), m_ref, acc_ref)
            last_two(sb_ref, sd_ref)

        return carry

    lax.fori_loop(1, n_q, query_tile, 0)


def _moba_attn(rel_t, qkv, neg, batch, seq):
    t = MOBA_TILE
    n_pairs = MOBA_HEADS // 2
    once = pl.Buffered(1)
    grid_spec = pltpu.PrefetchScalarGridSpec(
        num_scalar_prefetch=1,
        grid=(batch, n_pairs),
        in_specs=[
            pl.BlockSpec((seq, LANES), lambda b, p, rel: (b, COL_MQ // LANES + p),
                         pipeline_mode=once),
            pl.BlockSpec((seq, LANES), lambda b, p, rel: (b, COL_MK // LANES + p),
                         pipeline_mode=once),
            pl.BlockSpec((seq, LANES), lambda b, p, rel: (b, COL_MV // LANES + p),
                         pipeline_mode=once),
            pl.BlockSpec((seq, LANES), lambda b, p, rel: (b, p), pipeline_mode=once),
        ],
        out_specs=pl.BlockSpec((seq, LANES), lambda b, p, rel: (b, p)),
        scratch_shapes=[
            pltpu.VMEM((2, t, 2 * t), jnp.float32),
            pltpu.VMEM((seq, 2 * LANES), jnp.bfloat16),
            pltpu.VMEM((seq, 2 * LANES), jnp.bfloat16),
            pltpu.VMEM((2 * t, 2 * LANES), jnp.bfloat16),
            pltpu.VMEM((2 * t, t), jnp.float32),
            pltpu.VMEM((2 * t, t), jnp.float32),
            pltpu.VMEM((2 * t, t), jnp.float32),
            pltpu.VMEM((2 * t, LANES), jnp.float32),
            pltpu.VMEM((2 * t, 2 * LANES), jnp.float32),
        ],
    )
    return pl.pallas_call(
        _moba_attn_kernel,
        grid_spec=grid_spec,
        out_shape=jax.ShapeDtypeStruct((batch * seq, MOBA_HEADS * HEAD_DIM), jnp.bfloat16),
        compiler_params=_compiler_params(2, MOBA_VMEM_LIMIT_BYTES),
        name="moba_attn",
    )(rel_t, qkv, qkv, qkv, neg)


def _merge_kernel(ad_ref, am_ref, gl_ref, bg_ref, h_ref, wbd_ref, wbm_ref, wo_ref,
                  g_ref, b_ref, o_ref):
    for rows in _row_chains(o_ref.shape[0]):
        y_d = jnp.dot(ad_ref[rows, :], wbd_ref[...], preferred_element_type=jnp.float32)
        y_m = jnp.dot(am_ref[rows, :], wbm_ref[...], preferred_element_type=jnp.float32)
        gates = jax.nn.sigmoid(gl_ref[rows, :] + bg_ref[...])
        mixed = gates[:, :D_MODEL] * y_d + gates[:, D_MODEL:] * y_m
        out = jnp.dot(mixed.astype(jnp.bfloat16), wo_ref[...], preferred_element_type=jnp.float32)
        o_ref[rows, :] = _layer_norm(DEEPNORM_ALPHA * h_ref[rows, :] + out, g_ref[...], b_ref[...])


def _merge(attn_d, attn_m, gl, b_gate, h, wbd, wbm, wout, ln_g, ln_b):
    n_rows = h.shape[0]
    tm = ROW_TILE
    row = lambda cols: pl.BlockSpec((tm, cols), lambda i: (i, 0))
    return pl.pallas_call(
        _merge_kernel,
        grid=(n_rows // tm,),
        in_specs=[row(D_MODEL), row(MOBA_HEADS * HEAD_DIM), row(2 * D_MODEL),
                  _resident((1, 2 * D_MODEL)), row(D_MODEL),
                  _resident(wbd.shape), _resident(wbm.shape), _resident(wout.shape),
                  _resident((1, D_MODEL)), _resident((1, D_MODEL))],
        out_specs=row(D_MODEL),
        out_shape=jax.ShapeDtypeStruct((n_rows, D_MODEL), jnp.float32),
        compiler_params=_compiler_params(1),
        name="merge",
    )(attn_d, attn_m, gl, b_gate, h, wbd, wbm, wout, ln_g, ln_b)


def _xattn_kv_kernel(mem_ref, wk_ref, wv_ref, k_ref, v_ref):
    mb = mem_ref[...].astype(jnp.bfloat16)
    k_ref[...] = jnp.dot(mb, wk_ref[...], preferred_element_type=jnp.float32).astype(k_ref.dtype)
    v_ref[...] = jnp.dot(mb, wv_ref[...], preferred_element_type=jnp.float32).astype(v_ref.dtype)


def _xattn_kv(mem2, wk, wv, batch, n_mem):
    blk = pl.BlockSpec((n_mem, D_MODEL), lambda b: (b, 0))
    return pl.pallas_call(
        _xattn_kv_kernel,
        grid=(batch,),
        in_specs=[blk, _resident(wk.shape), _resident(wv.shape)],
        out_specs=[blk, blk],
        out_shape=[jax.ShapeDtypeStruct(mem2.shape, jnp.bfloat16)] * 2,
        compiler_params=_compiler_params(1),
        name="xattn_kv",
    )(mem2, wk, wv)


def _xattn_kernel(h_ref, wq_ref, k_ref, v_ref, wo_ref, g_ref, b_ref, o_ref):
    for rows in _row_chains(o_ref.shape[0]):
        h = h_ref[rows, :]
        q = jnp.dot(h.astype(jnp.bfloat16), wq_ref[...], preferred_element_type=jnp.float32)
        q = (q * XATTN_HEAD_DIM ** -0.5).astype(jnp.bfloat16)
        outs = []
        for hd in range(XATTN_HEADS):
            cols = slice(hd * XATTN_HEAD_DIM, (hd + 1) * XATTN_HEAD_DIM)
            s = _qk(q[:, cols], k_ref[:, cols])
            p = jnp.exp(s - jnp.max(s, axis=1, keepdims=True))
            p = p / jnp.sum(p, axis=1, keepdims=True)
            outs.append(jnp.dot(p.astype(jnp.bfloat16), v_ref[:, cols],
                                preferred_element_type=jnp.float32))
        o = jnp.concatenate(outs, axis=1).astype(jnp.bfloat16)
        xa = jnp.dot(o, wo_ref[...], preferred_element_type=jnp.float32)
        o_ref[rows, :] = _layer_norm(DEEPNORM_ALPHA * h + xa, g_ref[...], b_ref[...])


def _xattn(h1, wq, kx, vx, wo, ln_g, ln_b, batch, seq, n_mem):
    tm = ROW_TILE
    nq = seq // tm
    row = pl.BlockSpec((tm, D_MODEL), lambda b, i: (b * nq + i, 0))
    kv = pl.BlockSpec((n_mem, D_MODEL), lambda b, i: (b, 0))
    return pl.pallas_call(
        _xattn_kernel,
        grid=(batch, nq),
        in_specs=[row, _resident(wq.shape), kv, kv, _resident(wo.shape),
                  _resident((1, D_MODEL)), _resident((1, D_MODEL))],
        out_specs=row,
        out_shape=jax.ShapeDtypeStruct(h1.shape, jnp.float32),
        compiler_params=_compiler_params(2),
        name="xattn",
    )(h1, wq, kx, vx, wo, ln_g, ln_b)


def _mlp_kernel(h_ref, w1_ref, w2_ref, g_ref, b_ref, o_ref, hb_ref, acc_ref):
    j = pl.program_id(1)

    @pl.when(j == 0)
    def _():
        hb_ref[...] = h_ref[...].astype(jnp.bfloat16)
        acc_ref[...] = jnp.zeros(acc_ref.shape, jnp.float32)

    for rows in _row_chains(o_ref.shape[0]):
        u = jnp.dot(hb_ref[rows, :], w1_ref[...], preferred_element_type=jnp.float32)
        u = jnp.square(jnp.maximum(u, 0.0)).astype(jnp.bfloat16)
        acc_ref[rows, :] += jnp.dot(u, w2_ref[...], preferred_element_type=jnp.float32)

    @pl.when(j == pl.num_programs(1) - 1)
    def _():
        o_ref[...] = _layer_norm(DEEPNORM_ALPHA * h_ref[...] + acc_ref[...],
                                 g_ref[...], b_ref[...])


def _mlp(h2, w1, w2, ln_g, ln_b):
    n_rows = h2.shape[0]
    tm = MLP_ROW_TILE
    row = pl.BlockSpec((tm, D_MODEL), lambda i, j: (i, 0))
    return pl.pallas_call(
        _mlp_kernel,
        grid=(n_rows // tm, D_FF // FF_TILE),
        in_specs=[row,
                  pl.BlockSpec((D_MODEL, FF_TILE), lambda i, j: (0, j)),
                  pl.BlockSpec((FF_TILE, D_MODEL), lambda i, j: (j, 0)),
                  _resident((1, D_MODEL)), _resident((1, D_MODEL))],
        out_specs=row,
        out_shape=jax.ShapeDtypeStruct(h2.shape, jnp.float32),
        scratch_shapes=[pltpu.VMEM((tm, D_MODEL), jnp.bfloat16),
                        pltpu.VMEM((tm, D_MODEL), jnp.float32)],
        compiler_params=_compiler_params(2),
        name="mlp",
    )(h2, w1, w2, ln_g, ln_b)


def kernel(x, mem, ln_in_g, ln_in_b, rel_table, w_in, b_gate, lam_q1, lam_k1, lam_q2, lam_k2,
           diff_sub_g, w_br_diff, w_br_moba, w_out, ln1_g, ln1_b, wq_x, wk_x, wv_x, wo_x,
           ln2_g, ln2_b, w_ff1, w_ff2, ln3_g, ln3_b):
    batch, seq, d = x.shape
    n_mem = mem.shape[1]
    assert d == D_MODEL and w_in.shape == (DEPTH, D_MODEL, W_IN_COLS)
    assert seq % ROW_TILE == 0 and seq % DIFF_TILE == 0 and seq % SEL_TILE == 0
    assert (batch * seq) % MLP_ROW_TILE == 0
    assert seq % MOBA_TILE == 0 and MOBA_TILE % MOBA_BLOCK == 0
    assert seq // MOBA_BLOCK <= LANES // 2 and MOBA_TILE >= REL_FAR_DIST <= DIFF_TILE
    bf = lambda w: w.astype(jnp.bfloat16)
    vec = lambda v: v.reshape(1, -1).astype(jnp.float32)
    l = 0

    x2 = x.reshape(batch * seq, d)
    h, qkv, mqf, gl, kmean = _ln_inproj(x2, vec(ln_in_g), vec(ln_in_b), bf(w_in[l]))

    n_blocks = seq // MOBA_BLOCK
    km = kmean.reshape(batch, n_blocks, MOBA_HEADS // 2, 2, HEAD_DIM)
    km = jnp.pad(km, ((0, 0), (0, LANES // 2 - n_blocks), (0, 0), (0, 0), (0, 0)))
    eye2 = jnp.eye(2, dtype=jnp.float32)
    km2 = jnp.einsum('bnpad,ac->bpcnad', km, eye2).reshape(batch, MOBA_HEADS // 2, LANES, LANES)
    neg = _moba_select(mqf, km2, batch, seq)

    rel_t = rel_table.T.astype(jnp.float32)
    lam_p = jnp.concatenate([lam_q1[l:l + 1], lam_k1[l:l + 1], lam_q2[l:l + 1], lam_k2[l:l + 1]],
                            axis=0).astype(jnp.float32)
    attn_d = _diff_attn(rel_t, qkv, lam_p, vec(diff_sub_g[l]), batch, seq)
    attn_m = _moba_attn(rel_t, qkv, neg, batch, seq)

    h1 = _merge(attn_d, attn_m, gl, vec(b_gate[l]), h, bf(w_br_diff[l]), bf(w_br_moba[l]),
                bf(w_out[l]), vec(ln1_g[l]), vec(ln1_b[l]))

    kx, vx = _xattn_kv(mem.reshape(batch * n_mem, d), bf(wk_x[l]), bf(wv_x[l]), batch, n_mem)
    h2 = _xattn(h1, bf(wq_x[l]), kx, vx, bf(wo_x[l]), vec(ln2_g[l]), vec(ln2_b[l]),
                batch, seq, n_mem)

    h3 = _mlp(h2, bf(w_ff1[l]), bf(w_ff2[l]), vec(ln3_g[l]), vec(ln3_b[l]))
    return h3.reshape(batch, seq, d)
```

```python
import functools
import math

import numpy as np
import jax
import jax.numpy as jnp
from jax import lax
from jax.experimental import pallas as pl
from jax.experimental.pallas import tpu as pltpu

D_MODEL = 1024
HEAD_DIM = 64
DIFF_HEADS = 8
MOBA_HEADS = 8
MOBA_BLOCK = 256
MOBA_TOPK = 3
XATTN_HEADS = 4
XATTN_HEAD_DIM = D_MODEL // XATTN_HEADS
D_FF = 4 * D_MODEL
REL_BUCKETS = 32
REL_MAX_DIST = 128
LN_EPS = 1e-5
DEPTH = 1
DEEPNORM_ALPHA = (2.0 * DEPTH) ** 0.25
LAM_INIT = 0.8 - 0.6 * math.exp(-0.3 * 0)
LOG2_E = math.log2(math.e)

COL_DQ = 0
COL_DK = 1024
COL_DV = 2048
COL_MQ = 3072
COL_MK = 3584
COL_MV = 4096
COL_GATE = 4608
N_QKV_COLS = 4608
W_IN_COLS = 6656

LANES = 128
VMEM_LIMIT_BYTES = 56 * 1024 * 1024
MOBA_VMEM_LIMIT_BYTES = 62 * 1024 * 1024

ROW_TILE = 512
DIFF_TILE = 512
MOBA_TILE = 512
SEL_TILE = 512
FF_TILE = 1024
PROJ_CHUNK = 512
FAR_UNROLL = 16
ROW_CHAINS = 2
MLP_ROW_TILE = 1024

MASK_NEG = -1e30


def _rel_thresholds():
    max_exact = REL_BUCKETS // 2
    n = np.arange(0, 4 * REL_MAX_DIST)
    nf = np.maximum(n, 1).astype(np.float32)
    large = max_exact + (np.log(nf / max_exact) / math.log(REL_MAX_DIST / max_exact)
                         * (REL_BUCKETS - max_exact)).astype(np.int32)
    large = np.minimum(large, REL_BUCKETS - 1)
    bucket = np.where(n < max_exact, n, large)
    thr = [int(np.argmax(bucket >= k)) for k in range(REL_BUCKETS)]
    assert all(bucket[t] == k for k, t in enumerate(thr))
    return thr


REL_THRESHOLDS = _rel_thresholds()
REL_FAR_DIST = REL_THRESHOLDS[-1]


def _layer_norm(z, g, b):
    mu = jnp.mean(z, axis=-1, keepdims=True)
    zc = z - mu
    var = jnp.mean(zc * zc, axis=-1, keepdims=True)
    return zc * lax.rsqrt(var + LN_EPS) * g + b


def _compiler_params(n_grid, vmem_limit_bytes=VMEM_LIMIT_BYTES):
    return pltpu.CompilerParams(
        dimension_semantics=("arbitrary",) * n_grid,
        vmem_limit_bytes=vmem_limit_bytes,
    )


def _row_chains(n_rows, chains=ROW_CHAINS):
    step = n_rows // chains
    return [pl.ds(r0, step) for r0 in range(0, n_rows, step)]


def _resident(shape):
    nd = len(shape)
    return pl.BlockSpec(shape, lambda *_: (0,) * nd, pipeline_mode=pl.Buffered(1))


def _ln_inproj_kernel(x_ref, g_ref, b_ref, w_ref, h_ref, qkv_ref, mqf_ref, gl_ref, kmean_ref):
    h = _layer_norm(x_ref[...], g_ref[...], b_ref[...])
    h_ref[...] = h
    hb = h.astype(jnp.bfloat16)
    q_scale = HEAD_DIM ** -0.5 * LOG2_E
    for c0 in range(0, W_IN_COLS, PROJ_CHUNK):
        acc = jnp.dot(hb, w_ref[:, c0:c0 + PROJ_CHUNK], preferred_element_type=jnp.float32)
        if c0 < COL_GATE:
            is_q = c0 < COL_DK or COL_MQ <= c0 < COL_MK
            if COL_MQ <= c0 < COL_MK:
                mqf_ref[:, c0 - COL_MQ:c0 - COL_MQ + PROJ_CHUNK] = acc
            if COL_MK <= c0 < COL_MV:
                nblk = acc.shape[0] // MOBA_BLOCK
                km = jnp.mean(acc.reshape(nblk, MOBA_BLOCK, PROJ_CHUNK), axis=1)
                kmean_ref[0, :, c0 - COL_MK:c0 - COL_MK + PROJ_CHUNK] = km
            val = acc * q_scale if is_q else acc
            qkv_ref[:, c0:c0 + PROJ_CHUNK] = val.astype(jnp.bfloat16)
        else:
            gl_ref[:, c0 - COL_GATE:c0 - COL_GATE + PROJ_CHUNK] = acc


def _ln_inproj(x2, ln_g, ln_b, w_in_bf):
    n_rows = x2.shape[0]
    tm = ROW_TILE
    nblk = tm // MOBA_BLOCK
    grid = (n_rows // tm,)
    row = lambda cols: pl.BlockSpec((tm, cols), lambda i: (i, 0))
    return pl.pallas_call(
        _ln_inproj_kernel,
        grid=grid,
        in_specs=[row(D_MODEL), _resident((1, D_MODEL)), _resident((1, D_MODEL)),
                  _resident((D_MODEL, W_IN_COLS))],
        out_specs=[row(D_MODEL), row(N_QKV_COLS), row(MOBA_HEADS * HEAD_DIM), row(2 * D_MODEL),
                   pl.BlockSpec((1, nblk, MOBA_HEADS * HEAD_DIM), lambda i: (i, 0, 0))],
        out_shape=[
            jax.ShapeDtypeStruct((n_rows, D_MODEL), jnp.float32),
            jax.ShapeDtypeStruct((n_rows, N_QKV_COLS), jnp.bfloat16),
            jax.ShapeDtypeStruct((n_rows, MOBA_HEADS * HEAD_DIM), jnp.float32),
            jax.ShapeDtypeStruct((n_rows, 2 * D_MODEL), jnp.float32),
            jax.ShapeDtypeStruct((n_rows // tm, nblk, MOBA_HEADS * HEAD_DIM), jnp.float32),
        ],
        compiler_params=_compiler_params(1),
        name="ln_inproj",
    )(x2, ln_g, ln_b, w_in_bf)


def _moba_select_kernel(q_ref, km_ref, neg_ref, *, n_blocks):
    t = q_ref.shape[0]
    half = LANES // 2
    row0 = pl.program_id(1) * t
    shape3 = (2, half, t)
    blk = lax.broadcasted_iota(jnp.int32, shape3, 1)
    cur = (row0 + lax.broadcasted_iota(jnp.int32, shape3, 2)) // MOBA_BLOCK
    blk_f = blk.astype(jnp.float32)
    ninf = jnp.float32(-jnp.inf)
    for pair in range(MOBA_HEADS // 2):
        q = q_ref[:, pair * LANES:(pair + 1) * LANES]
        gate = lax.dot_general(km_ref[0, pair], q, (((1,), (1,)), ((), ())),
                               preferred_element_type=jnp.float32,
                               precision=lax.Precision.HIGHEST)
        g = jnp.where((blk < cur) & (blk < n_blocks), gate.reshape(shape3), ninf)
        neg = jnp.full(shape3, MASK_NEG, jnp.float32)
        for rank in range(MOBA_TOPK):
            mx = jnp.max(g, axis=1, keepdims=True)
            ix = jnp.min(jnp.where(g == mx, blk_f, float(half)), axis=1, keepdims=True)
            pick = blk_f == ix
            neg = jnp.where(pick & (rank < cur), 0.0, neg)
            g = jnp.where(pick, ninf, g)
        neg_ref[:, pair * LANES:(pair + 1) * LANES] = neg.reshape(LANES, t).T.astype(neg_ref.dtype)


def _moba_select(mqf, km2, batch, seq):
    t = SEL_TILE
    nq = seq // t
    n_blocks = seq // MOBA_BLOCK
    return pl.pallas_call(
        functools.partial(_moba_select_kernel, n_blocks=n_blocks),
        grid=(batch, nq),
        in_specs=[pl.BlockSpec((t, MOBA_HEADS * HEAD_DIM), lambda b, i: (b * nq + i, 0)),
                  pl.BlockSpec((1, MOBA_HEADS // 2, LANES, LANES), lambda b, i: (b, 0, 0, 0))],
        out_specs=pl.BlockSpec((t, MOBA_HEADS * HEAD_DIM), lambda b, i: (b * nq + i, 0)),
        out_shape=jax.ShapeDtypeStruct(mqf.shape, jnp.bfloat16),
        compiler_params=_compiler_params(2),
        name="moba_select",
    )(mqf, km2)


def _fill_bias_table(tab_ref, rel_ref, head, t):
    far = rel_ref[head, REL_BUCKETS - 1]
    r = lax.broadcasted_iota(jnp.int32, (t, 2 * t), 0)
    c = lax.broadcasted_iota(jnp.int32, (t, 2 * t), 1)
    dist = r - c + t
    bias = jnp.full((t, 2 * t), (rel_ref[head, 0] - far) * LOG2_E, jnp.float32)
    for k in range(1, REL_BUCKETS):
        bias = jnp.where(dist >= REL_THRESHOLDS[k], (rel_ref[head, k] - far) * LOG2_E, bias)
    tab_ref[...] = jnp.where(dist >= 0, bias, MASK_NEG)


def _softmax_step(s, v_aug, m_ref, acc_ref):
    m_prev = m_ref[...]
    m_new = jnp.maximum(m_prev, jnp.max(s, axis=1, keepdims=True))
    alpha = jnp.exp2(m_prev - m_new)
    p = jnp.exp2(s - jnp.tile(m_new, (1, s.shape[1] // LANES)))
    pv = jnp.dot(p.astype(jnp.bfloat16), v_aug, preferred_element_type=jnp.float32)
    acc_ref[...] = jnp.tile(alpha, (1, 2)) * acc_ref[...] + pv
    m_ref[...] = m_new


def _far_tiles(n, last, logits_of, values_of, sa_ref, sb_ref, m_ref, acc_ref):
    def run(first, count):
        for u in range(count):
            cur, nxt = (sa_ref, sb_ref) if u % 2 == 0 else (sb_ref, sa_ref)
            nxt[...] = logits_of(jnp.minimum(first + u + 1, last))
            _softmax_step(cur[...], values_of(first + u), m_ref, acc_ref)

    groups = lax.shift_right_logical(n, FAR_UNROLL.bit_length() - 1)

    def group(i, carry):
        run(i * FAR_UNROLL, FAR_UNROLL)
        return carry

    lax.fori_loop(0, groups, group, 0)
    done = groups * FAR_UNROLL
    count = FAR_UNROLL // 2
    while count >= 2:
        @pl.when(((n - done) & count) != 0)
        def _(done=done, count=count):
            run(done, count)
        done = done + ((n - done) & count)
        count //= 2


def _init_softmax(m_ref, acc_ref):
    m_ref[...] = jnp.full(m_ref.shape, MASK_NEG, jnp.float32)
    acc_ref[...] = jnp.zeros(acc_ref.shape, jnp.float32)


def _fill_values_aug(vaug_ref, v_ref):
    vaug_ref[:, 0:LANES] = v_ref[...]
    vaug_ref[:, LANES:2 * LANES] = jnp.ones(v_ref.shape, vaug_ref.dtype)


def _qk(q2, k):
    return lax.dot_general(q2, k, (((1,), (1,)), ((), ())), preferred_element_type=jnp.float32)


def _split_halves(q):
    lane = lax.broadcasted_iota(jnp.int32, q.shape, 1)
    zero = jnp.zeros_like(q)
    return jnp.concatenate([jnp.where(lane < LANES // 2, q, zero),
                            jnp.where(lane < LANES // 2, zero, q)], axis=0)


def _diff_attn_kernel(rel_ref, q_ref, k_ref, v_ref, lam_ref, subg_ref, o_ref,
                      tab_ref, vaug_ref, sa_ref, sb_ref, sd_ref, m_ref, acc_ref):
    t = DIFF_TILE
    n_q = q_ref.shape[0] // t
    head = pl.program_id(1)
    _fill_bias_table(tab_ref, rel_ref, head, t)
    _fill_values_aug(vaug_ref, v_ref)
    lam_p = lam_ref[...]
    lam = (jnp.exp(jnp.sum(lam_p[0:1] * lam_p[1:2], axis=1, keepdims=True))
           - jnp.exp(jnp.sum(lam_p[2:3] * lam_p[3:4], axis=1, keepdims=True)) + LAM_INIT)

    def rows_of(i):
        return pl.ds(pl.multiple_of(i * t, t), t)

    def queries_of(i):
        return _split_halves(q_ref[rows_of(i), :])

    def values_of(j):
        return vaug_ref[rows_of(j), :]

    def with_bias(s_ref, col0):
        return (s_ref[...].reshape(2, t, t) + tab_ref[:, col0:col0 + t][None]).reshape(2 * t, t)

    def finish(qi):
        o = acc_ref[:, 0:LANES] / acc_ref[:, LANES:2 * LANES]
        o = o[:t] - lam * o[t:]
        o = o * lax.rsqrt(jnp.mean(o * o, axis=1, keepdims=True) + LN_EPS) * subg_ref[...]
        o_ref[rows_of(qi), :] = (o * (1.0 - LAM_INIT)).astype(o_ref.dtype)

    def first_logits_of_next(qi):
        nxt = jnp.minimum(qi + 1, n_q - 1)
        sa_ref[...] = _qk(queries_of(nxt), k_ref[rows_of(0), :])

    q2 = queries_of(0)
    sd_ref[0:t, :] = _qk(q2[0:t], k_ref[rows_of(0), :])
    sd_ref[t:2 * t, :] = _qk(q2[t:2 * t], k_ref[rows_of(0), :])
    _init_softmax(m_ref, acc_ref)
    first_logits_of_next(0)
    _softmax_step(with_bias(sd_ref, t), values_of(0), m_ref, acc_ref)
    finish(0)

    def query_tile(qi, carry):
        q2 = queries_of(qi)

        def logits_of(j):
            return _qk(q2, k_ref[rows_of(j), :])

        def last_two(cur_ref, other_ref):
            other_ref[...] = logits_of(qi)
            _softmax_step(with_bias(cur_ref, 0), values_of(qi - 1), m_ref, acc_ref)
            first_logits_of_next(qi)
            _softmax_step(with_bias(other_ref, t), values_of(qi), m_ref, acc_ref)
            finish(qi)

        n_far = qi - 1
        _init_softmax(m_ref, acc_ref)
        _far_tiles(n_far, n_far, logits_of, values_of, sa_ref, sb_ref, m_ref, acc_ref)

        @pl.when((n_far & 1) == 0)
        def _():
            last_two(sa_ref, sb_ref)

        @pl.when((n_far & 1) == 1)
        def _():
            sb_ref[...] = logits_of(n_far)
            _softmax_step(sa_ref[...], values_of(n_far - 1), m_ref, acc_ref)
            last_two(sb_ref, sd_ref)

        return carry

    lax.fori_loop(1, n_q, query_tile, 0)


def _diff_attn(rel_t, qkv, lam_p, sub_g, batch, seq):
    t = DIFF_TILE
    once = pl.Buffered(1)
    grid_spec = pltpu.PrefetchScalarGridSpec(
        num_scalar_prefetch=1,
        grid=(batch, DIFF_HEADS),
        in_specs=[
            pl.BlockSpec((seq, LANES), lambda b, h, rel: (b, COL_DQ // LANES + h)),
            pl.BlockSpec((seq, LANES), lambda b, h, rel: (b, COL_DK // LANES + h)),
            pl.BlockSpec((seq, LANES), lambda b, h, rel: (b, COL_DV // LANES + h),
                         pipeline_mode=once),
            pl.BlockSpec((4, HEAD_DIM), lambda b, h, rel: (0, 0)),
            pl.BlockSpec((1, LANES), lambda b, h, rel: (0, 0)),
        ],
        out_specs=pl.BlockSpec((seq, LANES), lambda b, h, rel: (b, h)),
        scratch_shapes=[
            pltpu.VMEM((t, 2 * t), jnp.float32),
            pltpu.VMEM((seq, 2 * LANES), jnp.bfloat16),
            pltpu.VMEM((2 * t, t), jnp.float32),
            pltpu.VMEM((2 * t, t), jnp.float32),
            pltpu.VMEM((2 * t, t), jnp.float32),
            pltpu.VMEM((2 * t, LANES), jnp.float32),
            pltpu.VMEM((2 * t, 2 * LANES), jnp.float32),
        ],
    )
    return pl.pallas_call(
        _diff_attn_kernel,
        grid_spec=grid_spec,
        out_shape=jax.ShapeDtypeStruct((batch * seq, DIFF_HEADS * LANES), jnp.bfloat16),
        compiler_params=_compiler_params(2),
        name="diff_attn",
    )(rel_t, qkv, qkv, qkv, lam_p, sub_g)


def _moba_attn_kernel(rel_ref, q_ref, k_ref, v_ref, neg_ref, o_ref,
                      tab_ref, kaug_ref, vaug_ref, lhs_ref, sa_ref, sb_ref, sd_ref,
                      m_ref, acc_ref):
    t = MOBA_TILE
    seq = k_ref.shape[0]
    n_q = seq // t
    half = LANES // 2
    blocks_per_tile = t // MOBA_BLOCK
    pair = pl.program_id(1)

    _fill_bias_table(tab_ref.at[0], rel_ref, DIFF_HEADS + 2 * pair, t)
    _fill_bias_table(tab_ref.at[1], rel_ref, DIFF_HEADS + 2 * pair + 1, t)
    _fill_values_aug(vaug_ref, v_ref)
    kaug_ref[:, 0:LANES] = k_ref[...]
    key_blk = lax.broadcasted_iota(jnp.int32, (seq, LANES), 0) // MOBA_BLOCK
    lane_k = lax.broadcasted_iota(jnp.int32, (seq, LANES), 1)
    kaug_ref[:, LANES:2 * LANES] = jnp.where(key_blk == lane_k, 1.0, 0.0).astype(kaug_ref.dtype)

    def rows_of(i):
        return pl.ds(pl.multiple_of(i * t, t), t)

    def keys_of(j):
        return kaug_ref[rows_of(j), :]

    def values_of(j):
        return vaug_ref[rows_of(j), :]

    def masks_of(i):
        neg = neg_ref[rows_of(i), :].astype(jnp.float32)
        return jnp.concatenate([neg, pltpu.roll(neg, half, axis=1)], axis=0)

    def lhs_of(i):
        q2 = _split_halves(q_ref[rows_of(i), :])
        return jnp.concatenate([q2, masks_of(i).astype(q2.dtype)], axis=1)

    def with_bias(s_ref, col0):
        return (s_ref[...].reshape(2, t, t) + tab_ref[:, :, col0:col0 + t]).reshape(2 * t, t)

    def own_logits(i, s_ref):
        s = with_bias(s_ref, t).reshape(2, t, t)
        neg2 = masks_of(i)
        lane = lax.broadcasted_iota(jnp.int32, (2 * t, LANES), 1)
        cols = []
        for cb in range(blocks_per_tile):
            part = s[:, :, cb * MOBA_BLOCK:(cb + 1) * MOBA_BLOCK]
            if cb < blocks_per_tile - 1:
                chosen = jnp.sum(jnp.where(lane == i * blocks_per_tile + cb, neg2, 0.0),
                                 axis=1, keepdims=True).reshape(2, t, 1)
                later = (cb + 1) * MOBA_BLOCK
                part = jnp.concatenate([part[:, :later], part[:, later:] + chosen[:, later:]],
                                       axis=1)
            cols.append(part)
        return jnp.concatenate(cols, axis=2).reshape(2 * t, t)

    def finish(qi):
        o = acc_ref[:, 0:LANES] / acc_ref[:, LANES:2 * LANES]
        lane_o = lax.broadcasted_iota(jnp.int32, (t, LANES), 1)
        o_ref[rows_of(qi), :] = jnp.where(lane_o < half, o[:t], o[t:]).astype(o_ref.dtype)

    def first_logits_of_next(qi):
        sa_ref[...] = _qk(lhs_of(jnp.minimum(qi + 1, n_q - 1)), keys_of(0))

    q2 = _split_halves(q_ref[rows_of(0), :])
    k_own = keys_of(0)[:, 0:LANES]
    sd_ref[0:t, :] = _qk(q2[0:t], k_own)
    sd_ref[t:2 * t, :] = _qk(q2[t:2 * t], k_own)
    _init_softmax(m_ref, acc_ref)
    first_logits_of_next(0)
    _softmax_step(own_logits(0, sd_ref), values_of(0), m_ref, acc_ref)
    finish(0)

    def query_tile(qi, carry):
        lhs_ref[...] = lhs_of(qi)

        def logits_of(j):
            return _qk(lhs_ref[...], keys_of(j))

        def last_two(cur_ref, other_ref):
            other_ref[...] = _qk(lhs_ref[:, 0:LANES], keys_of(qi)[:, 0:LANES])
            _softmax_step(with_bias(cur_ref, 0), values_of(qi - 1), m_ref, acc_ref)
            first_logits_of_next(qi)
            _softmax_step(own_logits(qi, other_ref), values_of(qi), m_ref, acc_ref)
            finish(qi)

        n_far = qi - 1
        _init_softmax(m_ref, acc_ref)
        _far_tiles(n_far, n_far, logits_of, values_of, sa_ref, sb_ref, m_ref, acc_ref)

        @pl.when((n_far & 1) == 0)
        def _():
            last_two(sa_ref, sb_ref)

        @pl.when((n_far & 1) == 1)
        def _():
            sb_ref[...] = logits_of(n_far)
            _softmax_step(sa_ref[...], values_of(n_far - 1), m_ref, acc_ref)
            last_two(sb_ref, sd_ref)

        return carry

    lax.fori_loop(1, n_q, query_tile, 0)


def _moba_attn(rel_t, qkv, neg, batch, seq):
    t = MOBA_TILE
    n_pairs = MOBA_HEADS // 2
    once = pl.Buffered(1)
    grid_spec = pltpu.PrefetchScalarGridSpec(
        num_scalar_prefetch=1,
        grid=(batch, n_pairs),
        in_specs=[
            pl.BlockSpec((seq, LANES), lambda b, p, rel: (b, COL_MQ // LANES + p),
                         pipeline_mode=once),
            pl.BlockSpec((seq, LANES), lambda b, p, rel: (b, COL_MK // LANES + p),
                         pipeline_mode=once),
            pl.BlockSpec((seq, LANES), lambda b, p, rel: (b, COL_MV // LANES + p),
                         pipeline_mode=once),
            pl.BlockSpec((seq, LANES), lambda b, p, rel: (b, p), pipeline_mode=once),
        ],
        out_specs=pl.BlockSpec((seq, LANES), lambda b, p, rel: (b, p)),
        scratch_shapes=[
            pltpu.VMEM((2, t, 2 * t), jnp.float32),
            pltpu.VMEM((seq, 2 * LANES), jnp.bfloat16),
            pltpu.VMEM((seq, 2 * LANES), jnp.bfloat16),
            pltpu.VMEM((2 * t, 2 * LANES), jnp.bfloat16),
            pltpu.VMEM((2 * t, t), jnp.float32),
            pltpu.VMEM((2 * t, t), jnp.float32),
            pltpu.VMEM((2 * t, t), jnp.float32),
            pltpu.VMEM((2 * t, LANES), jnp.float32),
            pltpu.VMEM((2 * t, 2 * LANES), jnp.float32),
        ],
    )
    return pl.pallas_call(
        _moba_attn_kernel,
        grid_spec=grid_spec,
        out_shape=jax.ShapeDtypeStruct((batch * seq, MOBA_HEADS * HEAD_DIM), jnp.bfloat16),
        compiler_params=_compiler_params(2, MOBA_VMEM_LIMIT_BYTES),
        name="moba_attn",
    )(rel_t, qkv, qkv, qkv, neg)


def _merge_kernel(ad_ref, am_ref, gl_ref, bg_ref, h_ref, wbd_ref, wbm_ref, wo_ref,
                  g_ref, b_ref, o_ref):
    for rows in _row_chains(o_ref.shape[0]):
        y_d = jnp.dot(ad_ref[rows, :], wbd_ref[...], preferred_element_type=jnp.float32)
        y_m = jnp.dot(am_ref[rows, :], wbm_ref[...], preferred_element_type=jnp.float32)
        gates = jax.nn.sigmoid(gl_ref[rows, :] + bg_ref[...])
        mixed = gates[:, :D_MODEL] * y_d + gates[:, D_MODEL:] * y_m
        out = jnp.dot(mixed.astype(jnp.bfloat16), wo_ref[...], preferred_element_type=jnp.float32)
        o_ref[rows, :] = _layer_norm(DEEPNORM_ALPHA * h_ref[rows, :] + out, g_ref[...], b_ref[...])


def _merge(attn_d, attn_m, gl, b_gate, h, wbd, wbm, wout, ln_g, ln_b):
    n_rows = h.shape[0]
    tm = ROW_TILE
    row = lambda cols: pl.BlockSpec((tm, cols), lambda i: (i, 0))
    return pl.pallas_call(
        _merge_kernel,
        grid=(n_rows // tm,),
        in_specs=[row(D_MODEL), row(MOBA_HEADS * HEAD_DIM), row(2 * D_MODEL),
                  _resident((1, 2 * D_MODEL)), row(D_MODEL),
                  _resident(wbd.shape), _resident(wbm.shape), _resident(wout.shape),
                  _resident((1, D_MODEL)), _resident((1, D_MODEL))],
        out_specs=row(D_MODEL),
        out_shape=jax.ShapeDtypeStruct((n_rows, D_MODEL), jnp.float32),
        compiler_params=_compiler_params(1),
        name="merge",
    )(attn_d, attn_m, gl, b_gate, h, wbd, wbm, wout, ln_g, ln_b)


def _xattn_kv_kernel(mem_ref, wk_ref, wv_ref, k_ref, v_ref):
    mb = mem_ref[...].astype(jnp.bfloat16)
    k_ref[...] = jnp.dot(mb, wk_ref[...], preferred_element_type=jnp.float32).astype(k_ref.dtype)
    v_ref[...] = jnp.dot(mb, wv_ref[...], preferred_element_type=jnp.float32).astype(v_ref.dtype)


def _xattn_kv(mem2, wk, wv, batch, n_mem):
    blk = pl.BlockSpec((n_mem, D_MODEL), lambda b: (b, 0))
    return pl.pallas_call(
        _xattn_kv_kernel,
        grid=(batch,),
        in_specs=[blk, _resident(wk.shape), _resident(wv.shape)],
        out_specs=[blk, blk],
        out_shape=[jax.ShapeDtypeStruct(mem2.shape, jnp.bfloat16)] * 2,
        compiler_params=_compiler_params(1),
        name="xattn_kv",
    )(mem2, wk, wv)


def _xattn_kernel(h_ref, wq_ref, k_ref, v_ref, wo_ref, g_ref, b_ref, o_ref):
    heads = [slice(hd * XATTN_HEAD_DIM, (hd + 1) * XATTN_HEAD_DIM) for hd in range(XATTN_HEADS)]
    h = h_ref[...]
    q = jnp.dot(h.astype(jnp.bfloat16), wq_ref[...], preferred_element_type=jnp.float32)
    q = (q * XATTN_HEAD_DIM ** -0.5).astype(jnp.bfloat16)
    logits = [_qk(q[:, cols], k_ref[:, cols]) for cols in heads]
    probs = []
    for s in logits:
        p = jnp.exp(s - jnp.max(s, axis=1, keepdims=True))
        probs.append((p / jnp.sum(p, axis=1, keepdims=True)).astype(jnp.bfloat16))
    outs = [jnp.dot(p, v_ref[:, cols], preferred_element_type=jnp.float32)
            for p, cols in zip(probs, heads)]
    o = jnp.concatenate(outs, axis=1).astype(jnp.bfloat16)
    xa = jnp.dot(o, wo_ref[...], preferred_element_type=jnp.float32)
    o_ref[...] = _layer_norm(DEEPNORM_ALPHA * h + xa, g_ref[...], b_ref[...])


def _xattn(h1, wq, kx, vx, wo, ln_g, ln_b, batch, seq, n_mem):
    tm = ROW_TILE
    nq = seq // tm
    row = pl.BlockSpec((tm, D_MODEL), lambda b, i: (b * nq + i, 0))
    kv = pl.BlockSpec((n_mem, D_MODEL), lambda b, i: (b, 0))
    return pl.pallas_call(
        _xattn_kernel,
        grid=(batch, nq),
        in_specs=[row, _resident(wq.shape), kv, kv, _resident(wo.shape),
                  _resident((1, D_MODEL)), _resident((1, D_MODEL))],
        out_specs=row,
        out_shape=jax.ShapeDtypeStruct(h1.shape, jnp.float32),
        compiler_params=_compiler_params(2),
        name="xattn",
    )(h1, wq, kx, vx, wo, ln_g, ln_b)


def _mlp_kernel(h_ref, w1_ref, w2_ref, g_ref, b_ref, o_ref, hb_ref, acc_ref):
    j = pl.program_id(1)

    @pl.when(j == 0)
    def _():
        hb_ref[...] = h_ref[...].astype(jnp.bfloat16)
        acc_ref[...] = jnp.zeros(acc_ref.shape, jnp.float32)

    for rows in _row_chains(o_ref.shape[0]):
        u = jnp.dot(hb_ref[rows, :], w1_ref[...], preferred_element_type=jnp.float32)
        u = jnp.square(jnp.maximum(u, 0.0)).astype(jnp.bfloat16)
        acc_ref[rows, :] += jnp.dot(u, w2_ref[...], preferred_element_type=jnp.float32)

    @pl.when(j == pl.num_programs(1) - 1)
    def _():
        o_ref[...] = _layer_norm(DEEPNORM_ALPHA * h_ref[...] + acc_ref[...],
                                 g_ref[...], b_ref[...])


def _mlp(h2, w1, w2, ln_g, ln_b):
    n_rows = h2.shape[0]
    tm = MLP_ROW_TILE
    row = pl.BlockSpec((tm, D_MODEL), lambda i, j: (i, 0))
    return pl.pallas_call(
        _mlp_kernel,
        grid=(n_rows // tm, D_FF // FF_TILE),
        in_specs=[row,
                  pl.BlockSpec((D_MODEL, FF_TILE), lambda i, j: (0, j)),
                  pl.BlockSpec((FF_TILE, D_MODEL), lambda i, j: (j, 0)),
                  _resident((1, D_MODEL)), _resident((1, D_MODEL))],
        out_specs=row,
        out_shape=jax.ShapeDtypeStruct(h2.shape, jnp.float32),
        scratch_shapes=[pltpu.VMEM((tm, D_MODEL), jnp.bfloat16),
                        pltpu.VMEM((tm, D_MODEL), jnp.float32)],
        compiler_params=_compiler_params(2),
        name="mlp",
    )(h2, w1, w2, ln_g, ln_b)


def kernel(x, mem, ln_in_g, ln_in_b, rel_table, w_in, b_gate, lam_q1, lam_k1, lam_q2, lam_k2,
           diff_sub_g, w_br_diff, w_br_moba, w_out, ln1_g, ln1_b, wq_x, wk_x, wv_x, wo_x,
           ln2_g, ln2_b, w_ff1, w_ff2, ln3_g, ln3_b):
    batch, seq, d = x.shape
    n_mem = mem.shape[1]
    assert d == D_MODEL and w_in.shape == (DEPTH, D_MODEL, W_IN_COLS)
    assert seq % ROW_TILE == 0 and seq % DIFF_TILE == 0 and seq % SEL_TILE == 0
    assert (batch * seq) % MLP_ROW_TILE == 0
    assert seq % MOBA_TILE == 0 and MOBA_TILE % MOBA_BLOCK == 0
    assert seq // MOBA_BLOCK <= LANES // 2 and MOBA_TILE >= REL_FAR_DIST <= DIFF_TILE
    bf = lambda w: w.astype(jnp.bfloat16)
    vec = lambda v: v.reshape(1, -1).astype(jnp.float32)
    l = 0

    x2 = x.reshape(batch * seq, d)
    h, qkv, mqf, gl, kmean = _ln_inproj(x2, vec(ln_in_g), vec(ln_in_b), bf(w_in[l]))

    n_blocks = seq // MOBA_BLOCK
    km = kmean.reshape(batch, n_blocks, MOBA_HEADS // 2, 2, HEAD_DIM)
    km = jnp.pad(km, ((0, 0), (0, LANES // 2 - n_blocks), (0, 0), (0, 0), (0, 0)))
    eye2 = jnp.eye(2, dtype=jnp.float32)
    km2 = jnp.einsum('bnpad,ac->bpcnad', km, eye2).reshape(batch, MOBA_HEADS // 2, LANES, LANES)
    neg = _moba_select(mqf, km2, batch, seq)

    rel_t = rel_table.T.astype(jnp.float32)
    lam_p = jnp.concatenate([lam_q1[l:l + 1], lam_k1[l:l + 1], lam_q2[l:l + 1], lam_k2[l:l + 1]],
                            axis=0).astype(jnp.float32)
    attn_d = _diff_attn(rel_t, qkv, lam_p, vec(diff_sub_g[l]), batch, seq)
    attn_m = _moba_attn(rel_t, qkv, neg, batch, seq)

    h1 = _merge(attn_d, attn_m, gl, vec(b_gate[l]), h, bf(w_br_diff[l]), bf(w_br_moba[l]),
                bf(w_out[l]), vec(ln1_g[l]), vec(ln1_b[l]))

    kx, vx = _xattn_kv(mem.reshape(batch * n_mem, d), bf(wk_x[l]), bf(wv_x[l]), batch, n_mem)
    h2 = _xattn(h1, bf(wq_x[l]), kx, vx, bf(wo_x[l]), vec(ln2_g[l]), vec(ln2_b[l]),
                batch, seq, n_mem)

    h3 = _mlp(h2, bf(w_ff1[l]), bf(w_ff2[l]), vec(ln3_g[l]), vec(ln3_b[l]))
    return h3.reshape(batch, seq, d)
```

```python
import functools
import math

import numpy as np
import jax
import jax.numpy as jnp
from jax import lax
from jax.experimental import pallas as pl
from jax.experimental.pallas import tpu as pltpu

D_MODEL = 1024
HEAD_DIM = 64
DIFF_HEADS = 8
MOBA_HEADS = 8
MOBA_BLOCK = 256
MOBA_TOPK = 3
XATTN_HEADS = 4
XATTN_HEAD_DIM = D_MODEL // XATTN_HEADS
D_FF = 4 * D_MODEL
REL_BUCKETS = 32
REL_MAX_DIST = 128
LN_EPS = 1e-5
DEPTH = 1
DEEPNORM_ALPHA = (2.0 * DEPTH) ** 0.25
LAM_INIT = 0.8 - 0.6 * math.exp(-0.3 * 0)
LOG2_E = math.log2(math.e)

COL_DQ = 0
COL_DK = 1024
COL_DV = 2048
COL_MQ = 3072
COL_MK = 3584
COL_MV = 4096
COL_GATE = 4608
N_QKV_COLS = 4608
W_IN_COLS = 6656

LANES = 128
VMEM_LIMIT_BYTES = 56 * 1024 * 1024
MOBA_VMEM_LIMIT_BYTES = 62 * 1024 * 1024

ROW_TILE = 512
DIFF_TILE = 512
MOBA_TILE = 512
SEL_TILE = 512
FF_TILE = 1024
PROJ_CHUNK = 512
FAR_UNROLL = 16
ROW_CHAINS = 2
MLP_ROW_TILE = 1024

MASK_NEG = -1e30


def _rel_thresholds():
    max_exact = REL_BUCKETS // 2
    n = np.arange(0, 4 * REL_MAX_DIST)
    nf = np.maximum(n, 1).astype(np.float32)
    large = max_exact + (np.log(nf / max_exact) / math.log(REL_MAX_DIST / max_exact)
                         * (REL_BUCKETS - max_exact)).astype(np.int32)
    large = np.minimum(large, REL_BUCKETS - 1)
    bucket = np.where(n < max_exact, n, large)
    thr = [int(np.argmax(bucket >= k)) for k in range(REL_BUCKETS)]
    assert all(bucket[t] == k for k, t in enumerate(thr))
    return thr


REL_THRESHOLDS = _rel_thresholds()
REL_FAR_DIST = REL_THRESHOLDS[-1]


def _layer_norm(z, g, b):
    mu = jnp.mean(z, axis=-1, keepdims=True)
    zc = z - mu
    var = jnp.mean(zc * zc, axis=-1, keepdims=True)
    return zc * lax.rsqrt(var + LN_EPS) * g + b


def _compiler_params(n_grid, vmem_limit_bytes=VMEM_LIMIT_BYTES):
    return pltpu.CompilerParams(
        dimension_semantics=("arbitrary",) * n_grid,
        vmem_limit_bytes=vmem_limit_bytes,
    )


def _row_chains(n_rows, chains=ROW_CHAINS):
    step = n_rows // chains
    return [pl.ds(r0, step) for r0 in range(0, n_rows, step)]


def _resident(shape):
    nd = len(shape)
    return pl.BlockSpec(shape, lambda *_: (0,) * nd, pipeline_mode=pl.Buffered(1))


def _ln_inproj_kernel(x_ref, g_ref, b_ref, w_ref, h_ref, qkv_ref, mqf_ref, gl_ref, kmean_ref):
    h = _layer_norm(x_ref[...], g_ref[...], b_ref[...])
    h_ref[...] = h
    hb = h.astype(jnp.bfloat16)
    q_scale = HEAD_DIM ** -0.5 * LOG2_E
    for c0 in range(0, W_IN_COLS, PROJ_CHUNK):
        acc = jnp.dot(hb, w_ref[:, c0:c0 + PROJ_CHUNK], preferred_element_type=jnp.float32)
        if c0 < COL_GATE:
            is_q = c0 < COL_DK or COL_MQ <= c0 < COL_MK
            if COL_MQ <= c0 < COL_MK:
                mqf_ref[:, c0 - COL_MQ:c0 - COL_MQ + PROJ_CHUNK] = acc
            if COL_MK <= c0 < COL_MV:
                nblk = acc.shape[0] // MOBA_BLOCK
                km = jnp.mean(acc.reshape(nblk, MOBA_BLOCK, PROJ_CHUNK), axis=1)
                kmean_ref[0, :, c0 - COL_MK:c0 - COL_MK + PROJ_CHUNK] = km
            val = acc * q_scale if is_q else acc
            qkv_ref[:, c0:c0 + PROJ_CHUNK] = val.astype(jnp.bfloat16)
        else:
            gl_ref[:, c0 - COL_GATE:c0 - COL_GATE + PROJ_CHUNK] = acc


def _ln_inproj(x2, ln_g, ln_b, w_in_bf):
    n_rows = x2.shape[0]
    tm = ROW_TILE
    nblk = tm // MOBA_BLOCK
    grid = (n_rows // tm,)
    row = lambda cols: pl.BlockSpec((tm, cols), lambda i: (i, 0))
    return pl.pallas_call(
        _ln_inproj_kernel,
        grid=grid,
        in_specs=[row(D_MODEL), _resident((1, D_MODEL)), _resident((1, D_MODEL)),
                  _resident((D_MODEL, W_IN_COLS))],
        out_specs=[row(D_MODEL), row(N_QKV_COLS), row(MOBA_HEADS * HEAD_DIM), row(2 * D_MODEL),
                   pl.BlockSpec((1, nblk, MOBA_HEADS * HEAD_DIM), lambda i: (i, 0, 0))],
        out_shape=[
            jax.ShapeDtypeStruct((n_rows, D_MODEL), jnp.float32),
            jax.ShapeDtypeStruct((n_rows, N_QKV_COLS), jnp.bfloat16),
            jax.ShapeDtypeStruct((n_rows, MOBA_HEADS * HEAD_DIM), jnp.float32),
            jax.ShapeDtypeStruct((n_rows, 2 * D_MODEL), jnp.float32),
            jax.ShapeDtypeStruct((n_rows // tm, nblk, MOBA_HEADS * HEAD_DIM), jnp.float32),
        ],
        compiler_params=_compiler_params(1),
        name="ln_inproj",
    )(x2, ln_g, ln_b, w_in_bf)


def _moba_select_kernel(q_ref, km_ref, neg_ref, *, n_blocks):
    t = q_ref.shape[0]
    half = LANES // 2
    row0 = pl.program_id(1) * t
    shape3 = (2, half, t)
    blk = lax.broadcasted_iota(jnp.int32, shape3, 1)
    cur = (row0 + lax.broadcasted_iota(jnp.int32, shape3, 2)) // MOBA_BLOCK
    blk_f = blk.astype(jnp.float32)
    ninf = jnp.float32(-jnp.inf)
    past = (blk < cur) & (blk < n_blocks)
    few_past = jnp.where(past, 0.0, MASK_NEG)
    for pair in range(MOBA_HEADS // 2):
        q = q_ref[:, pair * LANES:(pair + 1) * LANES]
        gate = lax.dot_general(km_ref[0, pair], q, (((1,), (1,)), ((), ())),
                               preferred_element_type=jnp.float32,
                               precision=lax.Precision.HIGHEST)
        g = jnp.where(past, gate.reshape(shape3), ninf)
        neg = jnp.full(shape3, MASK_NEG, jnp.float32)
        for rank in range(MOBA_TOPK):
            mx = jnp.max(g, axis=1, keepdims=True)
            ix = jnp.min(jnp.where(g == mx, blk_f, float(half)), axis=1, keepdims=True)
            pick = blk_f == ix
            neg = jnp.where(pick, 0.0, neg)
            g = jnp.where(pick, ninf, g)
        neg = jnp.where(cur >= MOBA_TOPK, neg, few_past)
        neg_ref[:, pair * LANES:(pair + 1) * LANES] = neg.reshape(LANES, t).T.astype(neg_ref.dtype)


def _moba_select(mqf, km2, batch, seq):
    t = SEL_TILE
    nq = seq // t
    n_blocks = seq // MOBA_BLOCK
    return pl.pallas_call(
        functools.partial(_moba_select_kernel, n_blocks=n_blocks),
        grid=(batch, nq),
        in_specs=[pl.BlockSpec((t, MOBA_HEADS * HEAD_DIM), lambda b, i: (b * nq + i, 0)),
                  pl.BlockSpec((1, MOBA_HEADS // 2, LANES, LANES), lambda b, i: (b, 0, 0, 0))],
        out_specs=pl.BlockSpec((t, MOBA_HEADS * HEAD_DIM), lambda b, i: (b * nq + i, 0)),
        out_shape=jax.ShapeDtypeStruct(mqf.shape, jnp.bfloat16),
        compiler_params=_compiler_params(2),
        name="moba_select",
    )(mqf, km2)


def _fill_bias_table(tab_ref, rel_ref, head, t):
    far = rel_ref[head, REL_BUCKETS - 1]
    r = lax.broadcasted_iota(jnp.int32, (t, 2 * t), 0)
    c = lax.broadcasted_iota(jnp.int32, (t, 2 * t), 1)
    dist = r - c + t
    bias = jnp.full((t, 2 * t), (rel_ref[head, 0] - far) * LOG2_E, jnp.float32)
    for k in range(1, REL_BUCKETS):
        bias = jnp.where(dist >= REL_THRESHOLDS[k], (rel_ref[head, k] - far) * LOG2_E, bias)
    tab_ref[...] = jnp.where(dist >= 0, bias, MASK_NEG)


def _softmax_step(s, v_aug, m_ref, acc_ref):
    m_prev = m_ref[...]
    m_new = jnp.maximum(m_prev, jnp.max(s, axis=1, keepdims=True))
    alpha = jnp.exp2(m_prev - m_new)
    p = jnp.exp2(s - jnp.tile(m_new, (1, s.shape[1] // LANES)))
    pv = jnp.dot(p.astype(jnp.bfloat16), v_aug, preferred_element_type=jnp.float32)
    acc_ref[...] = jnp.tile(alpha, (1, 2)) * acc_ref[...] + pv
    m_ref[...] = m_new


def _far_tiles(n, last, logits_of, values_of, sa_ref, sb_ref, m_ref, acc_ref):
    def run(first, count):
        for u in range(count):
            cur, nxt = (sa_ref, sb_ref) if u % 2 == 0 else (sb_ref, sa_ref)
            nxt[...] = logits_of(jnp.minimum(first + u + 1, last))
            _softmax_step(cur[...], values_of(first + u), m_ref, acc_ref)

    groups = lax.shift_right_logical(n, FAR_UNROLL.bit_length() - 1)

    def group(i, carry):
        run(i * FAR_UNROLL, FAR_UNROLL)
        return carry

    lax.fori_loop(0, groups, group, 0)
    done = groups * FAR_UNROLL
    count = FAR_UNROLL // 2
    while count >= 2:
        @pl.when(((n - done) & count) != 0)
        def _(done=done, count=count):
            run(done, count)
        done = done + ((n - done) & count)
        count //= 2


def _init_softmax(m_ref, acc_ref):
    m_ref[...] = jnp.full(m_ref.shape, MASK_NEG, jnp.float32)
    acc_ref[...] = jnp.zeros(acc_ref.shape, jnp.float32)


def _qk(q2, k):
    return lax.dot_general(q2, k, (((1,), (1,)), ((), ())), preferred_element_type=jnp.float32)


def _split_halves(q):
    lane = lax.broadcasted_iota(jnp.int32, q.shape, 1)
    zero = jnp.zeros_like(q)
    return jnp.concatenate([jnp.where(lane < LANES // 2, q, zero),
                            jnp.where(lane < LANES // 2, zero, q)], axis=0)


def _diff_attn_kernel(rel_ref, q_ref, k_ref, v_ref, lam_ref, subg_ref, o_ref,
                      tab_ref, vaug_ref, sa_ref, sb_ref, sd_ref, m_ref, acc_ref):
    t = DIFF_TILE
    n_q = q_ref.shape[0] // t
    head = pl.program_id(0)
    batch_idx = pl.program_id(1)

    @pl.when(batch_idx == 0)
    def _():
        _fill_bias_table(tab_ref, rel_ref, head, t)

    @pl.when((batch_idx == 0) & (head == 0))
    def _():
        vaug_ref[:, LANES:2 * LANES] = jnp.ones(v_ref.shape, vaug_ref.dtype)

    vaug_ref[:, 0:LANES] = v_ref[...]
    lam_p = lam_ref[...]
    lam = (jnp.exp(jnp.sum(lam_p[0:1] * lam_p[1:2], axis=1, keepdims=True))
           - jnp.exp(jnp.sum(lam_p[2:3] * lam_p[3:4], axis=1, keepdims=True)) + LAM_INIT)

    def rows_of(i):
        return pl.ds(pl.multiple_of(i * t, t), t)

    def queries_of(i):
        return _split_halves(q_ref[rows_of(i), :])

    def values_of(j):
        return vaug_ref[rows_of(j), :]

    def with_bias(s_ref, col0):
        return (s_ref[...].reshape(2, t, t) + tab_ref[:, col0:col0 + t][None]).reshape(2 * t, t)

    def finish(qi):
        o = acc_ref[:, 0:LANES] / acc_ref[:, LANES:2 * LANES]
        o = o[:t] - lam * o[t:]
        o = o * lax.rsqrt(jnp.mean(o * o, axis=1, keepdims=True) + LN_EPS) * subg_ref[...]
        o_ref[rows_of(qi), :] = (o * (1.0 - LAM_INIT)).astype(o_ref.dtype)

    def first_logits_of_next(qi):
        nxt = jnp.minimum(qi + 1, n_q - 1)
        sa_ref[...] = _qk(queries_of(nxt), k_ref[rows_of(0), :])

    q2 = queries_of(0)
    sd_ref[0:t, :] = _qk(q2[0:t], k_ref[rows_of(0), :])
    sd_ref[t:2 * t, :] = _qk(q2[t:2 * t], k_ref[rows_of(0), :])
    _init_softmax(m_ref, acc_ref)
    first_logits_of_next(0)
    _softmax_step(with_bias(sd_ref, t), values_of(0), m_ref, acc_ref)
    finish(0)

    def query_tile(qi, carry):
        q2 = queries_of(qi)

        def logits_of(j):
            return _qk(q2, k_ref[rows_of(j), :])

        def last_two(cur_ref, other_ref):
            other_ref[...] = logits_of(qi)
            _softmax_step(with_bias(cur_ref, 0), values_of(qi - 1), m_ref, acc_ref)
            first_logits_of_next(qi)
            _softmax_step(with_bias(other_ref, t), values_of(qi), m_ref, acc_ref)
            finish(qi)

        n_far = qi - 1
        _init_softmax(m_ref, acc_ref)
        _far_tiles(n_far, n_far, logits_of, values_of, sa_ref, sb_ref, m_ref, acc_ref)

        @pl.when((n_far & 1) == 0)
        def _():
            last_two(sa_ref, sb_ref)

        @pl.when((n_far & 1) == 1)
        def _():
            sb_ref[...] = logits_of(n_far)
            _softmax_step(sa_ref[...], values_of(n_far - 1), m_ref, acc_ref)
            last_two(sb_ref, sd_ref)

        return carry

    lax.fori_loop(1, n_q, query_tile, 0)


def _diff_attn(rel_t, qkv, lam_p, sub_g, batch, seq):
    t = DIFF_TILE
    once = pl.Buffered(1)
    grid_spec = pltpu.PrefetchScalarGridSpec(
        num_scalar_prefetch=1,
        grid=(DIFF_HEADS, batch),
        in_specs=[
            pl.BlockSpec((seq, LANES), lambda h, b, rel: (b, COL_DQ // LANES + h)),
            pl.BlockSpec((seq, LANES), lambda h, b, rel: (b, COL_DK // LANES + h)),
            pl.BlockSpec((seq, LANES), lambda h, b, rel: (b, COL_DV // LANES + h),
                         pipeline_mode=once),
            pl.BlockSpec((4, HEAD_DIM), lambda h, b, rel: (0, 0)),
            pl.BlockSpec((1, LANES), lambda h, b, rel: (0, 0)),
        ],
        out_specs=pl.BlockSpec((seq, LANES), lambda h, b, rel: (b, h)),
        scratch_shapes=[
            pltpu.VMEM((t, 2 * t), jnp.float32),
            pltpu.VMEM((seq, 2 * LANES), jnp.bfloat16),
            pltpu.VMEM((2 * t, t), jnp.float32),
            pltpu.VMEM((2 * t, t), jnp.float32),
            pltpu.VMEM((2 * t, t), jnp.float32),
            pltpu.VMEM((2 * t, LANES), jnp.float32),
            pltpu.VMEM((2 * t, 2 * LANES), jnp.float32),
        ],
    )
    return pl.pallas_call(
        _diff_attn_kernel,
        grid_spec=grid_spec,
        out_shape=jax.ShapeDtypeStruct((batch * seq, DIFF_HEADS * LANES), jnp.bfloat16),
        compiler_params=_compiler_params(2),
        name="diff_attn",
    )(rel_t, qkv, qkv, qkv, lam_p, sub_g)


def _moba_attn_kernel(rel_ref, q_ref, k_ref, v_ref, neg_ref, o_ref,
                      tab_ref, kaug_ref, vaug_ref, lhs_ref, sa_ref, sb_ref, sd_ref,
                      m_ref, acc_ref):
    t = MOBA_TILE
    seq = k_ref.shape[0]
    n_q = seq // t
    half = LANES // 2
    blocks_per_tile = t // MOBA_BLOCK
    pair = pl.program_id(0)
    batch_idx = pl.program_id(1)

    @pl.when(batch_idx == 0)
    def _():
        _fill_bias_table(tab_ref.at[0], rel_ref, DIFF_HEADS + 2 * pair, t)
        _fill_bias_table(tab_ref.at[1], rel_ref, DIFF_HEADS + 2 * pair + 1, t)

    @pl.when((batch_idx == 0) & (pair == 0))
    def _():
        vaug_ref[:, LANES:2 * LANES] = jnp.ones(v_ref.shape, vaug_ref.dtype)
        key_blk = lax.broadcasted_iota(jnp.int32, (seq, LANES), 0) // MOBA_BLOCK
        lane_k = lax.broadcasted_iota(jnp.int32, (seq, LANES), 1)
        kaug_ref[:, LANES:2 * LANES] = jnp.where(key_blk == lane_k, 1.0, 0.0).astype(kaug_ref.dtype)

    vaug_ref[:, 0:LANES] = v_ref[...]
    kaug_ref[:, 0:LANES] = k_ref[...]

    def rows_of(i):
        return pl.ds(pl.multiple_of(i * t, t), t)

    def keys_of(j):
        return kaug_ref[rows_of(j), :]

    def values_of(j):
        return vaug_ref[rows_of(j), :]

    def masks_of(i):
        neg = neg_ref[rows_of(i), :].astype(jnp.float32)
        return jnp.concatenate([neg, pltpu.roll(neg, half, axis=1)], axis=0)

    def lhs_of(i):
        q2 = _split_halves(q_ref[rows_of(i), :])
        return jnp.concatenate([q2, masks_of(i).astype(q2.dtype)], axis=1)

    def with_bias(s_ref, col0):
        return (s_ref[...].reshape(2, t, t) + tab_ref[:, :, col0:col0 + t]).reshape(2 * t, t)

    def own_logits(i, s_ref):
        s = with_bias(s_ref, t).reshape(2, t, t)
        neg2 = masks_of(i)
        lane = lax.broadcasted_iota(jnp.int32, (2 * t, LANES), 1)
        cols = []
        for cb in range(blocks_per_tile):
            part = s[:, :, cb * MOBA_BLOCK:(cb + 1) * MOBA_BLOCK]
            if cb < blocks_per_tile - 1:
                chosen = jnp.sum(jnp.where(lane == i * blocks_per_tile + cb, neg2, 0.0),
                                 axis=1, keepdims=True).reshape(2, t, 1)
                later = (cb + 1) * MOBA_BLOCK
                part = jnp.concatenate([part[:, :later], part[:, later:] + chosen[:, later:]],
                                       axis=1)
            cols.append(part)
        return jnp.concatenate(cols, axis=2).reshape(2 * t, t)

    def finish(qi):
        o = acc_ref[:, 0:LANES] / acc_ref[:, LANES:2 * LANES]
        lane_o = lax.broadcasted_iota(jnp.int32, (t, LANES), 1)
        o_ref[rows_of(qi), :] = jnp.where(lane_o < half, o[:t], o[t:]).astype(o_ref.dtype)

    def first_logits_of_next(qi):
        sa_ref[...] = _qk(lhs_of(jnp.minimum(qi + 1, n_q - 1)), keys_of(0))

    q2 = _split_halves(q_ref[rows_of(0), :])
    k_own = keys_of(0)[:, 0:LANES]
    sd_ref[0:t, :] = _qk(q2[0:t], k_own)
    sd_ref[t:2 * t, :] = _qk(q2[t:2 * t], k_own)
    _init_softmax(m_ref, acc_ref)
    first_logits_of_next(0)
    _softmax_step(own_logits(0, sd_ref), values_of(0), m_ref, acc_ref)
    finish(0)

    def query_tile(qi, carry):
        lhs_ref[...] = lhs_of(qi)

        def logits_of(j):
            return _qk(lhs_ref[...], keys_of(j))

        def last_two(cur_ref, other_ref):
            other_ref[...] = _qk(lhs_ref[:, 0:LANES], keys_of(qi)[:, 0:LANES])
            _softmax_step(with_bias(cur_ref, 0), values_of(qi - 1), m_ref, acc_ref)
            first_logits_of_next(qi)
            _softmax_step(own_logits(qi, other_ref), values_of(qi), m_ref, acc_ref)
            finish(qi)

        n_far = qi - 1
        _init_softmax(m_ref, acc_ref)
        _far_tiles(n_far, n_far, logits_of, values_of, sa_ref, sb_ref, m_ref, acc_ref)

        @pl.when((n_far & 1) == 0)
        def _():
            last_two(sa_ref, sb_ref)

        @pl.when((n_far & 1) == 1)
        def _():
            sb_ref[...] = logits_of(n_far)
            _softmax_step(sa_ref[...], values_of(n_far - 1), m_ref, acc_ref)
            last_two(sb_ref, sd_ref)

        return carry

    lax.fori_loop(1, n_q, query_tile, 0)


def _moba_attn(rel_t, qkv, neg, batch, seq):
    t = MOBA_TILE
    n_pairs = MOBA_HEADS // 2
    once = pl.Buffered(1)
    grid_spec = pltpu.PrefetchScalarGridSpec(
        num_scalar_prefetch=1,
        grid=(n_pairs, batch),
        in_specs=[
            pl.BlockSpec((seq, LANES), lambda p, b, rel: (b, COL_MQ // LANES + p),
                         pipeline_mode=once),
            pl.BlockSpec((seq, LANES), lambda p, b, rel: (b, COL_MK // LANES + p),
                         pipeline_mode=once),
            pl.BlockSpec((seq, LANES), lambda p, b, rel: (b, COL_MV // LANES + p),
                         pipeline_mode=once),
            pl.BlockSpec((seq, LANES), lambda p, b, rel: (b, p), pipeline_mode=once),
        ],
        out_specs=pl.BlockSpec((seq, LANES), lambda p, b, rel: (b, p)),
        scratch_shapes=[
            pltpu.VMEM((2, t, 2 * t), jnp.float32),
            pltpu.VMEM((seq, 2 * LANES), jnp.bfloat16),
            pltpu.VMEM((seq, 2 * LANES), jnp.bfloat16),
            pltpu.VMEM((2 * t, 2 * LANES), jnp.bfloat16),
            pltpu.VMEM((2 * t, t), jnp.float32),
            pltpu.VMEM((2 * t, t), jnp.float32),
            pltpu.VMEM((2 * t, t), jnp.float32),
            pltpu.VMEM((2 * t, LANES), jnp.float32),
            pltpu.VMEM((2 * t, 2 * LANES), jnp.float32),
        ],
    )
    return pl.pallas_call(
        _moba_attn_kernel,
        grid_spec=grid_spec,
        out_shape=jax.ShapeDtypeStruct((batch * seq, MOBA_HEADS * HEAD_DIM), jnp.bfloat16),
        compiler_params=_compiler_params(2, MOBA_VMEM_LIMIT_BYTES),
        name="moba_attn",
    )(rel_t, qkv, qkv, qkv, neg)


def _merge_kernel(ad_ref, am_ref, gl_ref, bg_ref, h_ref, wbd_ref, wbm_ref, wo_ref,
                  g_ref, b_ref, o_ref):
    for rows in _row_chains(o_ref.shape[0]):
        y_d = jnp.dot(ad_ref[rows, :], wbd_ref[...], preferred_element_type=jnp.float32)
        y_m = jnp.dot(am_ref[rows, :], wbm_ref[...], preferred_element_type=jnp.float32)
        gates = jax.nn.sigmoid(gl_ref[rows, :] + bg_ref[...])
        mixed = gates[:, :D_MODEL] * y_d + gates[:, D_MODEL:] * y_m
        out = jnp.dot(mixed.astype(jnp.bfloat16), wo_ref[...], preferred_element_type=jnp.float32)
        o_ref[rows, :] = _layer_norm(DEEPNORM_ALPHA * h_ref[rows, :] + out, g_ref[...], b_ref[...])


def _merge(attn_d, attn_m, gl, b_gate, h, wbd, wbm, wout, ln_g, ln_b):
    n_rows = h.shape[0]
    tm = ROW_TILE
    row = lambda cols: pl.BlockSpec((tm, cols), lambda i: (i, 0))
    return pl.pallas_call(
        _merge_kernel,
        grid=(n_rows // tm,),
        in_specs=[row(D_MODEL), row(MOBA_HEADS * HEAD_DIM), row(2 * D_MODEL),
                  _resident((1, 2 * D_MODEL)), row(D_MODEL),
                  _resident(wbd.shape), _resident(wbm.shape), _resident(wout.shape),
                  _resident((1, D_MODEL)), _resident((1, D_MODEL))],
        out_specs=row(D_MODEL),
        out_shape=jax.ShapeDtypeStruct((n_rows, D_MODEL), jnp.float32),
        compiler_params=_compiler_params(1),
        name="merge",
    )(attn_d, attn_m, gl, b_gate, h, wbd, wbm, wout, ln_g, ln_b)


def _xattn_kv_kernel(mem_ref, wk_ref, wv_ref, k_ref, v_ref):
    mb = mem_ref[...].astype(jnp.bfloat16)
    k_ref[...] = jnp.dot(mb, wk_ref[...], preferred_element_type=jnp.float32).astype(k_ref.dtype)
    v_ref[...] = jnp.dot(mb, wv_ref[...], preferred_element_type=jnp.float32).astype(v_ref.dtype)


def _xattn_kv(mem2, wk, wv, batch, n_mem):
    blk = pl.BlockSpec((n_mem, D_MODEL), lambda b: (b, 0))
    return pl.pallas_call(
        _xattn_kv_kernel,
        grid=(batch,),
        in_specs=[blk, _resident(wk.shape), _resident(wv.shape)],
        out_specs=[blk, blk],
        out_shape=[jax.ShapeDtypeStruct(mem2.shape, jnp.bfloat16)] * 2,
        compiler_params=_compiler_params(1),
        name="xattn_kv",
    )(mem2, wk, wv)


def _xattn_kernel(h_ref, wq_ref, k_ref, v_ref, wo_ref, g_ref, b_ref, o_ref):
    heads = [slice(hd * XATTN_HEAD_DIM, (hd + 1) * XATTN_HEAD_DIM) for hd in range(XATTN_HEADS)]
    h = h_ref[...]
    q = jnp.dot(h.astype(jnp.bfloat16), wq_ref[...], preferred_element_type=jnp.float32)
    q = (q * XATTN_HEAD_DIM ** -0.5).astype(jnp.bfloat16)
    logits = [_qk(q[:, cols], k_ref[:, cols]) for cols in heads]
    probs = []
    for s in logits:
        p = jnp.exp(s - jnp.max(s, axis=1, keepdims=True))
        probs.append((p / jnp.sum(p, axis=1, keepdims=True)).astype(jnp.bfloat16))
    outs = [jnp.dot(p, v_ref[:, cols], preferred_element_type=jnp.float32)
            for p, cols in zip(probs, heads)]
    o = jnp.concatenate(outs, axis=1).astype(jnp.bfloat16)
    xa = jnp.dot(o, wo_ref[...], preferred_element_type=jnp.float32)
    o_ref[...] = _layer_norm(DEEPNORM_ALPHA * h + xa, g_ref[...], b_ref[...])


def _xattn(h1, wq, kx, vx, wo, ln_g, ln_b, batch, seq, n_mem):
    tm = ROW_TILE
    nq = seq // tm
    row = pl.BlockSpec((tm, D_MODEL), lambda b, i: (b * nq + i, 0))
    kv = pl.BlockSpec((n_mem, D_MODEL), lambda b, i: (b, 0))
    return pl.pallas_call(
        _xattn_kernel,
        grid=(batch, nq),
        in_specs=[row, _resident(wq.shape), kv, kv, _resident(wo.shape),
                  _resident((1, D_MODEL)), _resident((1, D_MODEL))],
        out_specs=row,
        out_shape=jax.ShapeDtypeStruct(h1.shape, jnp.float32),
        compiler_params=_compiler_params(2),
        name="xattn",
    )(h1, wq, kx, vx, wo, ln_g, ln_b)


def _mlp_kernel(h_ref, w1_ref, w2_ref, g_ref, b_ref, o_ref):
    for rows in _row_chains(o_ref.shape[0]):
        h = h_ref[rows, :]
        hb = h.astype(jnp.bfloat16)
        ff = None
        for c0 in range(0, D_FF, FF_TILE):
            u = jnp.dot(hb, w1_ref[:, c0:c0 + FF_TILE], preferred_element_type=jnp.float32)
            u = jnp.square(jnp.maximum(u, 0.0)).astype(jnp.bfloat16)
            part = jnp.dot(u, w2_ref[c0:c0 + FF_TILE, :], preferred_element_type=jnp.float32)
            ff = part if ff is None else ff + part
        o_ref[rows, :] = _layer_norm(DEEPNORM_ALPHA * h + ff, g_ref[...], b_ref[...])


def _mlp(h2, w1, w2, ln_g, ln_b):
    n_rows = h2.shape[0]
    tm = MLP_ROW_TILE
    row = pl.BlockSpec((tm, D_MODEL), lambda i: (i, 0))
    return pl.pallas_call(
        _mlp_kernel,
        grid=(n_rows // tm,),
        in_specs=[row, _resident(w1.shape), _resident(w2.shape),
                  _resident((1, D_MODEL)), _resident((1, D_MODEL))],
        out_specs=row,
        out_shape=jax.ShapeDtypeStruct(h2.shape, jnp.float32),
        compiler_params=_compiler_params(1),
        name="mlp",
    )(h2, w1, w2, ln_g, ln_b)


def kernel(x, mem, ln_in_g, ln_in_b, rel_table, w_in, b_gate, lam_q1, lam_k1, lam_q2, lam_k2,
           diff_sub_g, w_br_diff, w_br_moba, w_out, ln1_g, ln1_b, wq_x, wk_x, wv_x, wo_x,
           ln2_g, ln2_b, w_ff1, w_ff2, ln3_g, ln3_b):
    batch, seq, d = x.shape
    n_mem = mem.shape[1]
    assert d == D_MODEL and w_in.shape == (DEPTH, D_MODEL, W_IN_COLS)
    assert seq % ROW_TILE == 0 and seq % DIFF_TILE == 0 and seq % SEL_TILE == 0
    assert (batch * seq) % MLP_ROW_TILE == 0
    assert seq % MOBA_TILE == 0 and MOBA_TILE % MOBA_BLOCK == 0
    assert seq // MOBA_BLOCK <= LANES // 2 and MOBA_TILE >= REL_FAR_DIST <= DIFF_TILE
    bf = lambda w: w.astype(jnp.bfloat16)
    vec = lambda v: v.reshape(1, -1).astype(jnp.float32)
    l = 0

    x2 = x.reshape(batch * seq, d)
    h, qkv, mqf, gl, kmean = _ln_inproj(x2, vec(ln_in_g), vec(ln_in_b), bf(w_in[l]))

    n_blocks = seq // MOBA_BLOCK
    km = kmean.reshape(batch, n_blocks, MOBA_HEADS // 2, 2, HEAD_DIM)
    km = jnp.pad(km, ((0, 0), (0, LANES // 2 - n_blocks), (0, 0), (0, 0), (0, 0)))
    eye2 = jnp.eye(2, dtype=jnp.float32)
    km2 = jnp.einsum('bnpad,ac->bpcnad', km, eye2).reshape(batch, MOBA_HEADS // 2, LANES, LANES)
    neg = _moba_select(mqf, km2, batch, seq)

    rel_t = rel_table.T.astype(jnp.float32)
    lam_p = jnp.concatenate([lam_q1[l:l + 1], lam_k1[l:l + 1], lam_q2[l:l + 1], lam_k2[l:l + 1]],
                            axis=0).astype(jnp.float32)
    attn_d = _diff_attn(rel_t, qkv, lam_p, vec(diff_sub_g[l]), batch, seq)
    attn_m = _moba_attn(rel_t, qkv, neg, batch, seq)

    h1 = _merge(attn_d, attn_m, gl, vec(b_gate[l]), h, bf(w_br_diff[l]), bf(w_br_moba[l]),
                bf(w_out[l]), vec(ln1_g[l]), vec(ln1_b[l]))

    kx, vx = _xattn_kv(mem.reshape(batch * n_mem, d), bf(wk_x[l]), bf(wv_x[l]), batch, n_mem)
    h2 = _xattn(h1, bf(wq_x[l]), kx, vx, bf(wo_x[l]), vec(ln2_g[l]), vec(ln2_b[l]),
                batch, seq, n_mem)

    h3 = _mlp(h2, bf(w_ff1[l]), bf(w_ff2[l]), vec(ln3_g[l]), vec(ln3_b[l]))
    return h3.reshape(batch, seq, d)
```

```python
import functools
import math

import numpy as np
import jax
import jax.numpy as jnp
from jax import lax
from jax.experimental import pallas as pl
from jax.experimental.pallas import tpu as pltpu

D_MODEL = 1024
HEAD_DIM = 64
DIFF_HEADS = 8
MOBA_HEADS = 8
MOBA_BLOCK = 256
MOBA_TOPK = 3
XATTN_HEADS = 4
XATTN_HEAD_DIM = D_MODEL // XATTN_HEADS
D_FF = 4 * D_MODEL
REL_BUCKETS = 32
REL_MAX_DIST = 128
LN_EPS = 1e-5
DEPTH = 1
DEEPNORM_ALPHA = (2.0 * DEPTH) ** 0.25
LAM_INIT = 0.8 - 0.6 * math.exp(-0.3 * 0)
LOG2_E = math.log2(math.e)

COL_DQ = 0
COL_DK = 1024
COL_DV = 2048
COL_MQ = 3072
COL_MK = 3584
COL_MV = 4096
COL_GATE = 4608
N_QKV_COLS = 4608
W_IN_COLS = 6656

LANES = 128
VMEM_LIMIT_BYTES = 56 * 1024 * 1024
MOBA_VMEM_LIMIT_BYTES = 62 * 1024 * 1024

ROW_TILE = 512
DIFF_TILE = 512
MOBA_TILE = 512
SEL_TILE = 512
FF_TILE = 1024
PROJ_CHUNK = 512
FAR_UNROLL = 16
ROW_CHAINS = 2
MLP_ROW_TILE = 1024

MASK_NEG = -1e30


def _rel_thresholds():
    max_exact = REL_BUCKETS // 2
    n = np.arange(0, 4 * REL_MAX_DIST)
    nf = np.maximum(n, 1).astype(np.float32)
    large = max_exact + (np.log(nf / max_exact) / math.log(REL_MAX_DIST / max_exact)
                         * (REL_BUCKETS - max_exact)).astype(np.int32)
    large = np.minimum(large, REL_BUCKETS - 1)
    bucket = np.where(n < max_exact, n, large)
    thr = [int(np.argmax(bucket >= k)) for k in range(REL_BUCKETS)]
    assert all(bucket[t] == k for k, t in enumerate(thr))
    return thr


REL_THRESHOLDS = _rel_thresholds()
REL_FAR_DIST = REL_THRESHOLDS[-1]


def _layer_norm(z, g, b):
    mu = jnp.mean(z, axis=-1, keepdims=True)
    zc = z - mu
    var = jnp.mean(zc * zc, axis=-1, keepdims=True)
    return zc * lax.rsqrt(var + LN_EPS) * g + b


def _compiler_params(n_grid, vmem_limit_bytes=VMEM_LIMIT_BYTES):
    return pltpu.CompilerParams(
        dimension_semantics=("arbitrary",) * n_grid,
        vmem_limit_bytes=vmem_limit_bytes,
    )


def _row_chains(n_rows, chains=ROW_CHAINS):
    step = n_rows // chains
    return [pl.ds(r0, step) for r0 in range(0, n_rows, step)]


def _resident(shape):
    nd = len(shape)
    return pl.BlockSpec(shape, lambda *_: (0,) * nd, pipeline_mode=pl.Buffered(1))


def _ln_inproj_kernel(x_ref, g_ref, b_ref, w_ref, h_ref, qkv_ref, mqf_ref, gl_ref, kmean_ref):
    q_scale = HEAD_DIM ** -0.5 * LOG2_E
    for blk in range(x_ref.shape[0] // MOBA_BLOCK):
        rows = pl.ds(blk * MOBA_BLOCK, MOBA_BLOCK)
        h = _layer_norm(x_ref[rows, :], g_ref[...], b_ref[...])
        h_ref[rows, :] = h
        hb = h.astype(jnp.bfloat16)
        for c0 in range(0, W_IN_COLS, PROJ_CHUNK):
            acc = jnp.dot(hb, w_ref[:, c0:c0 + PROJ_CHUNK], preferred_element_type=jnp.float32)
            if c0 < COL_GATE:
                is_q = c0 < COL_DK or COL_MQ <= c0 < COL_MK
                if COL_MQ <= c0 < COL_MK:
                    mqf_ref[rows, c0 - COL_MQ:c0 - COL_MQ + PROJ_CHUNK] = acc
                if COL_MK <= c0 < COL_MV:
                    kmean_ref[0, blk:blk + 1, c0 - COL_MK:c0 - COL_MK + PROJ_CHUNK] = (
                        jnp.mean(acc, axis=0, keepdims=True))
                val = acc * q_scale if is_q else acc
                qkv_ref[rows, c0:c0 + PROJ_CHUNK] = val.astype(jnp.bfloat16)
            else:
                gl_ref[rows, c0 - COL_GATE:c0 - COL_GATE + PROJ_CHUNK] = acc


def _ln_inproj(x2, ln_g, ln_b, w_in_bf):
    n_rows = x2.shape[0]
    tm = ROW_TILE
    nblk = tm // MOBA_BLOCK
    grid = (n_rows // tm,)
    row = lambda cols: pl.BlockSpec((tm, cols), lambda i: (i, 0))
    return pl.pallas_call(
        _ln_inproj_kernel,
        grid=grid,
        in_specs=[row(D_MODEL), _resident((1, D_MODEL)), _resident((1, D_MODEL)),
                  _resident((D_MODEL, W_IN_COLS))],
        out_specs=[row(D_MODEL), row(N_QKV_COLS), row(MOBA_HEADS * HEAD_DIM), row(2 * D_MODEL),
                   pl.BlockSpec((1, nblk, MOBA_HEADS * HEAD_DIM), lambda i: (i, 0, 0))],
        out_shape=[
            jax.ShapeDtypeStruct((n_rows, D_MODEL), jnp.float32),
            jax.ShapeDtypeStruct((n_rows, N_QKV_COLS), jnp.bfloat16),
            jax.ShapeDtypeStruct((n_rows, MOBA_HEADS * HEAD_DIM), jnp.float32),
            jax.ShapeDtypeStruct((n_rows, 2 * D_MODEL), jnp.float32),
            jax.ShapeDtypeStruct((n_rows // tm, nblk, MOBA_HEADS * HEAD_DIM), jnp.float32),
        ],
        compiler_params=_compiler_params(1),
        name="ln_inproj",
    )(x2, ln_g, ln_b, w_in_bf)


def _moba_select_kernel(q_ref, km_ref, neg_ref, *, n_blocks):
    t = q_ref.shape[0]
    half = LANES // 2
    row0 = pl.program_id(1) * t
    shape3 = (2, half, t)
    blk = lax.broadcasted_iota(jnp.int32, shape3, 1)
    cur = (row0 + lax.broadcasted_iota(jnp.int32, shape3, 2)) // MOBA_BLOCK
    blk_f = blk.astype(jnp.float32)
    ninf = jnp.float32(-jnp.inf)
    past = (blk < cur) & (blk < n_blocks)
    few_past = jnp.where(past, 0.0, MASK_NEG)
    for pair in range(MOBA_HEADS // 2):
        q = q_ref[:, pair * LANES:(pair + 1) * LANES]
        gate = lax.dot_general(km_ref[0, pair], q, (((1,), (1,)), ((), ())),
                               preferred_element_type=jnp.float32,
                               precision=lax.Precision.HIGHEST)
        g = jnp.where(past, gate.reshape(shape3), ninf)
        neg = jnp.full(shape3, MASK_NEG, jnp.float32)
        for rank in range(MOBA_TOPK):
            mx = jnp.max(g, axis=1, keepdims=True)
            ix = jnp.min(jnp.where(g == mx, blk_f, float(half)), axis=1, keepdims=True)
            pick = blk_f == ix
            neg = jnp.where(pick, 0.0, neg)
            g = jnp.where(pick, ninf, g)
        neg = jnp.where(cur >= MOBA_TOPK, neg, few_past)
        neg_ref[:, pair * LANES:(pair + 1) * LANES] = neg.reshape(LANES, t).T.astype(neg_ref.dtype)


def _moba_select(mqf, km2, batch, seq):
    t = SEL_TILE
    nq = seq // t
    n_blocks = seq // MOBA_BLOCK
    return pl.pallas_call(
        functools.partial(_moba_select_kernel, n_blocks=n_blocks),
        grid=(batch, nq),
        in_specs=[pl.BlockSpec((t, MOBA_HEADS * HEAD_DIM), lambda b, i: (b * nq + i, 0)),
                  pl.BlockSpec((1, MOBA_HEADS // 2, LANES, LANES), lambda b, i: (b, 0, 0, 0))],
        out_specs=pl.BlockSpec((t, MOBA_HEADS * HEAD_DIM), lambda b, i: (b * nq + i, 0)),
        out_shape=jax.ShapeDtypeStruct(mqf.shape, jnp.bfloat16),
        compiler_params=_compiler_params(2),
        name="moba_select",
    )(mqf, km2)


def _fill_bias_table(tab_ref, rel_ref, head, t):
    far = rel_ref[head, REL_BUCKETS - 1]
    r = lax.broadcasted_iota(jnp.int32, (t, 2 * t), 0)
    c = lax.broadcasted_iota(jnp.int32, (t, 2 * t), 1)
    dist = r - c + t
    bias = jnp.full((t, 2 * t), (rel_ref[head, 0] - far) * LOG2_E, jnp.float32)
    for k in range(1, REL_BUCKETS):
        bias = jnp.where(dist >= REL_THRESHOLDS[k], (rel_ref[head, k] - far) * LOG2_E, bias)
    tab_ref[...] = jnp.where(dist >= 0, bias, MASK_NEG)


def _softmax_step(s, v_aug, m_ref, acc_ref):
    m_prev = m_ref[...]
    m_new = jnp.maximum(m_prev, jnp.max(s, axis=1, keepdims=True))
    alpha = jnp.exp2(m_prev - m_new)
    p = jnp.exp2(s - jnp.tile(m_new, (1, s.shape[1] // LANES)))
    pv = jnp.dot(p.astype(jnp.bfloat16), v_aug, preferred_element_type=jnp.float32)
    acc_ref[...] = jnp.tile(alpha, (1, 2)) * acc_ref[...] + pv
    m_ref[...] = m_new


def _far_tiles(n, last, logits_of, values_of, sa_ref, sb_ref, m_ref, acc_ref):
    def run(first, count):
        for u in range(count):
            cur, nxt = (sa_ref, sb_ref) if u % 2 == 0 else (sb_ref, sa_ref)
            nxt[...] = logits_of(jnp.minimum(first + u + 1, last))
            _softmax_step(cur[...], values_of(first + u), m_ref, acc_ref)

    groups = lax.shift_right_logical(n, FAR_UNROLL.bit_length() - 1)

    def group(i, carry):
        run(i * FAR_UNROLL, FAR_UNROLL)
        return carry

    lax.fori_loop(0, groups, group, 0)
    done = groups * FAR_UNROLL
    count = FAR_UNROLL // 2
    while count >= 2:
        @pl.when(((n - done) & count) != 0)
        def _(done=done, count=count):
            run(done, count)
        done = done + ((n - done) & count)
        count //= 2


def _init_softmax(m_ref, acc_ref):
    m_ref[...] = jnp.full(m_ref.shape, MASK_NEG, jnp.float32)
    acc_ref[...] = jnp.zeros(acc_ref.shape, jnp.float32)


def _qk(q2, k):
    return lax.dot_general(q2, k, (((1,), (1,)), ((), ())), preferred_element_type=jnp.float32)


def _split_halves(q):
    lane = lax.broadcasted_iota(jnp.int32, q.shape, 1)
    zero = jnp.zeros_like(q)
    return jnp.concatenate([jnp.where(lane < LANES // 2, q, zero),
                            jnp.where(lane < LANES // 2, zero, q)], axis=0)


def _diff_attn_kernel(rel_ref, q_ref, k_ref, v_ref, lam_ref, subg_ref, o_ref,
                      tab_ref, vaug_ref, sa_ref, sb_ref, sd_ref, m_ref, acc_ref):
    t = DIFF_TILE
    n_q = q_ref.shape[0] // t
    head = pl.program_id(0)
    batch_idx = pl.program_id(1)

    @pl.when(batch_idx == 0)
    def _():
        _fill_bias_table(tab_ref, rel_ref, head, t)

    @pl.when((batch_idx == 0) & (head == 0))
    def _():
        vaug_ref[:, LANES:2 * LANES] = jnp.ones(v_ref.shape, vaug_ref.dtype)

    vaug_ref[:, 0:LANES] = v_ref[...]
    lam_p = lam_ref[...]
    lam = (jnp.exp(jnp.sum(lam_p[0:1] * lam_p[1:2], axis=1, keepdims=True))
           - jnp.exp(jnp.sum(lam_p[2:3] * lam_p[3:4], axis=1, keepdims=True)) + LAM_INIT)

    def rows_of(i):
        return pl.ds(pl.multiple_of(i * t, t), t)

    def queries_of(i):
        return _split_halves(q_ref[rows_of(i), :])

    def values_of(j):
        return vaug_ref[rows_of(j), :]

    def with_bias(s_ref, col0):
        return (s_ref[...].reshape(2, t, t) + tab_ref[:, col0:col0 + t][None]).reshape(2 * t, t)

    def finish(qi):
        o = acc_ref[:, 0:LANES] / acc_ref[:, LANES:2 * LANES]
        o = o[:t] - lam * o[t:]
        o = o * lax.rsqrt(jnp.mean(o * o, axis=1, keepdims=True) + LN_EPS) * subg_ref[...]
        o_ref[rows_of(qi), :] = (o * (1.0 - LAM_INIT)).astype(o_ref.dtype)

    def first_logits_of_next(qi):
        nxt = jnp.minimum(qi + 1, n_q - 1)
        sa_ref[...] = _qk(queries_of(nxt), k_ref[rows_of(0), :])

    q2 = queries_of(0)
    sd_ref[0:t, :] = _qk(q2[0:t], k_ref[rows_of(0), :])
    sd_ref[t:2 * t, :] = _qk(q2[t:2 * t], k_ref[rows_of(0), :])
    _init_softmax(m_ref, acc_ref)
    first_logits_of_next(0)
    _softmax_step(with_bias(sd_ref, t), values_of(0), m_ref, acc_ref)
    finish(0)

    def query_tile(qi, carry):
        q2 = queries_of(qi)

        def logits_of(j):
            return _qk(q2, k_ref[rows_of(j), :])

        def last_two(cur_ref, other_ref):
            other_ref[...] = logits_of(qi)
            _softmax_step(with_bias(cur_ref, 0), values_of(qi - 1), m_ref, acc_ref)
            first_logits_of_next(qi)
            _softmax_step(with_bias(other_ref, t), values_of(qi), m_ref, acc_ref)
            finish(qi)

        n_far = qi - 1
        _init_softmax(m_ref, acc_ref)
        _far_tiles(n_far, n_far, logits_of, values_of, sa_ref, sb_ref, m_ref, acc_ref)

        @pl.when((n_far & 1) == 0)
        def _():
            last_two(sa_ref, sb_ref)

        @pl.when((n_far & 1) == 1)
        def _():
            sb_ref[...] = logits_of(n_far)
            _softmax_step(sa_ref[...], values_of(n_far - 1), m_ref, acc_ref)
            last_two(sb_ref, sd_ref)

        return carry

    lax.fori_loop(1, n_q, query_tile, 0)


def _diff_attn(rel_t, qkv, lam_p, sub_g, batch, seq):
    t = DIFF_TILE
    once = pl.Buffered(1)
    grid_spec = pltpu.PrefetchScalarGridSpec(
        num_scalar_prefetch=1,
        grid=(DIFF_HEADS, batch),
        in_specs=[
            pl.BlockSpec((seq, LANES), lambda h, b, rel: (b, COL_DQ // LANES + h)),
            pl.BlockSpec((seq, LANES), lambda h, b, rel: (b, COL_DK // LANES + h)),
            pl.BlockSpec((seq, LANES), lambda h, b, rel: (b, COL_DV // LANES + h),
                         pipeline_mode=once),
            pl.BlockSpec((4, HEAD_DIM), lambda h, b, rel: (0, 0)),
            pl.BlockSpec((1, LANES), lambda h, b, rel: (0, 0)),
        ],
        out_specs=pl.BlockSpec((seq, LANES), lambda h, b, rel: (b, h)),
        scratch_shapes=[
            pltpu.VMEM((t, 2 * t), jnp.float32),
            pltpu.VMEM((seq, 2 * LANES), jnp.bfloat16),
            pltpu.VMEM((2 * t, t), jnp.float32),
            pltpu.VMEM((2 * t, t), jnp.float32),
            pltpu.VMEM((2 * t, t), jnp.float32),
            pltpu.VMEM((2 * t, LANES), jnp.float32),
            pltpu.VMEM((2 * t, 2 * LANES), jnp.float32),
        ],
    )
    return pl.pallas_call(
        _diff_attn_kernel,
        grid_spec=grid_spec,
        out_shape=jax.ShapeDtypeStruct((batch * seq, DIFF_HEADS * LANES), jnp.bfloat16),
        compiler_params=_compiler_params(2),
        name="diff_attn",
    )(rel_t, qkv, qkv, qkv, lam_p, sub_g)


def _moba_attn_kernel(rel_ref, q_ref, k_ref, v_ref, neg_ref, o_ref,
                      tab_ref, kaug_ref, vaug_ref, lhs_ref, sa_ref, sb_ref, sd_ref,
                      m_ref, acc_ref):
    t = MOBA_TILE
    seq = k_ref.shape[0]
    n_q = seq // t
    half = LANES // 2
    blocks_per_tile = t // MOBA_BLOCK
    pair = pl.program_id(0)
    batch_idx = pl.program_id(1)

    @pl.when(batch_idx == 0)
    def _():
        _fill_bias_table(tab_ref.at[0], rel_ref, DIFF_HEADS + 2 * pair, t)
        _fill_bias_table(tab_ref.at[1], rel_ref, DIFF_HEADS + 2 * pair + 1, t)

    @pl.when((batch_idx == 0) & (pair == 0))
    def _():
        vaug_ref[:, LANES:2 * LANES] = jnp.ones(v_ref.shape, vaug_ref.dtype)
        key_blk = lax.broadcasted_iota(jnp.int32, (seq, LANES), 0) // MOBA_BLOCK
        lane_k = lax.broadcasted_iota(jnp.int32, (seq, LANES), 1)
        kaug_ref[:, LANES:2 * LANES] = jnp.where(key_blk == lane_k, 1.0, 0.0).astype(kaug_ref.dtype)

    vaug_ref[:, 0:LANES] = v_ref[...]
    kaug_ref[:, 0:LANES] = k_ref[...]

    def rows_of(i):
        return pl.ds(pl.multiple_of(i * t, t), t)

    def keys_of(j):
        return kaug_ref[rows_of(j), :]

    def values_of(j):
        return vaug_ref[rows_of(j), :]

    def masks_of(i):
        neg = neg_ref[rows_of(i), :].astype(jnp.float32)
        return jnp.concatenate([neg, pltpu.roll(neg, half, axis=1)], axis=0)

    def lhs_of(i):
        q2 = _split_halves(q_ref[rows_of(i), :])
        return jnp.concatenate([q2, masks_of(i).astype(q2.dtype)], axis=1)

    def with_bias(s_ref, col0):
        return (s_ref[...].reshape(2, t, t) + tab_ref[:, :, col0:col0 + t]).reshape(2 * t, t)

    def own_logits(i, s_ref):
        s = with_bias(s_ref, t).reshape(2, t, t)
        neg2 = masks_of(i)
        lane = lax.broadcasted_iota(jnp.int32, (2 * t, LANES), 1)
        cols = []
        for cb in range(blocks_per_tile):
            part = s[:, :, cb * MOBA_BLOCK:(cb + 1) * MOBA_BLOCK]
            if cb < blocks_per_tile - 1:
                chosen = jnp.sum(jnp.where(lane == i * blocks_per_tile + cb, neg2, 0.0),
                                 axis=1, keepdims=True).reshape(2, t, 1)
                later = (cb + 1) * MOBA_BLOCK
                part = jnp.concatenate([part[:, :later], part[:, later:] + chosen[:, later:]],
                                       axis=1)
            cols.append(part)
        return jnp.concatenate(cols, axis=2).reshape(2 * t, t)

    def finish(qi):
        o = acc_ref[:, 0:LANES] / acc_ref[:, LANES:2 * LANES]
        lane_o = lax.broadcasted_iota(jnp.int32, (t, LANES), 1)
        o_ref[rows_of(qi), :] = jnp.where(lane_o < half, o[:t], o[t:]).astype(o_ref.dtype)

    def first_logits_of_next(qi):
        sa_ref[...] = _qk(lhs_of(jnp.minimum(qi + 1, n_q - 1)), keys_of(0))

    q2 = _split_halves(q_ref[rows_of(0), :])
    k_own = keys_of(0)[:, 0:LANES]
    sd_ref[0:t, :] = _qk(q2[0:t], k_own)
    sd_ref[t:2 * t, :] = _qk(q2[t:2 * t], k_own)
    _init_softmax(m_ref, acc_ref)
    first_logits_of_next(0)
    _softmax_step(own_logits(0, sd_ref), values_of(0), m_ref, acc_ref)
    finish(0)

    def query_tile(qi, carry):
        lhs_ref[...] = lhs_of(qi)

        def logits_of(j):
            return _qk(lhs_ref[...], keys_of(j))

        def last_two(cur_ref, other_ref):
            other_ref[...] = _qk(lhs_ref[:, 0:LANES], keys_of(qi)[:, 0:LANES])
            _softmax_step(with_bias(cur_ref, 0), values_of(qi - 1), m_ref, acc_ref)
            first_logits_of_next(qi)
            _softmax_step(own_logits(qi, other_ref), values_of(qi), m_ref, acc_ref)
            finish(qi)

        n_far = qi - 1
        _init_softmax(m_ref, acc_ref)
        _far_tiles(n_far, n_far, logits_of, values_of, sa_ref, sb_ref, m_ref, acc_ref)

        @pl.when((n_far & 1) == 0)
        def _():
            last_two(sa_ref, sb_ref)

        @pl.when((n_far & 1) == 1)
        def _():
            sb_ref[...] = logits_of(n_far)
            _softmax_step(sa_ref[...], values_of(n_far - 1), m_ref, acc_ref)
            last_two(sb_ref, sd_ref)

        return carry

    lax.fori_loop(1, n_q, query_tile, 0)


def _moba_attn(rel_t, qkv, neg, batch, seq):
    t = MOBA_TILE
    n_pairs = MOBA_HEADS // 2
    once = pl.Buffered(1)
    grid_spec = pltpu.PrefetchScalarGridSpec(
        num_scalar_prefetch=1,
        grid=(n_pairs, batch),
        in_specs=[
            pl.BlockSpec((seq, LANES), lambda p, b, rel: (b, COL_MQ // LANES + p),
                         pipeline_mode=once),
            pl.BlockSpec((seq, LANES), lambda p, b, rel: (b, COL_MK // LANES + p),
                         pipeline_mode=once),
            pl.BlockSpec((seq, LANES), lambda p, b, rel: (b, COL_MV // LANES + p),
                         pipeline_mode=once),
            pl.BlockSpec((seq, LANES), lambda p, b, rel: (b, p), pipeline_mode=once),
        ],
        out_specs=pl.BlockSpec((seq, LANES), lambda p, b, rel: (b, p)),
        scratch_shapes=[
            pltpu.VMEM((2, t, 2 * t), jnp.float32),
            pltpu.VMEM((seq, 2 * LANES), jnp.bfloat16),
            pltpu.VMEM((seq, 2 * LANES), jnp.bfloat16),
            pltpu.VMEM((2 * t, 2 * LANES), jnp.bfloat16),
            pltpu.VMEM((2 * t, t), jnp.float32),
            pltpu.VMEM((2 * t, t), jnp.float32),
            pltpu.VMEM((2 * t, t), jnp.float32),
            pltpu.VMEM((2 * t, LANES), jnp.float32),
            pltpu.VMEM((2 * t, 2 * LANES), jnp.float32),
        ],
    )
    return pl.pallas_call(
        _moba_attn_kernel,
        grid_spec=grid_spec,
        out_shape=jax.ShapeDtypeStruct((batch * seq, MOBA_HEADS * HEAD_DIM), jnp.bfloat16),
        compiler_params=_compiler_params(2, MOBA_VMEM_LIMIT_BYTES),
        name="moba_attn",
    )(rel_t, qkv, qkv, qkv, neg)


def _merge_kernel(ad_ref, am_ref, gl_ref, bg_ref, h_ref, wbd_ref, wbm_ref, wo_ref,
                  g_ref, b_ref, o_ref):
    for rows in _row_chains(o_ref.shape[0]):
        y_d = jnp.dot(ad_ref[rows, :], wbd_ref[...], preferred_element_type=jnp.float32)
        y_m = jnp.dot(am_ref[rows, :], wbm_ref[...], preferred_element_type=jnp.float32)
        gates = jax.nn.sigmoid(gl_ref[rows, :] + bg_ref[...])
        mixed = gates[:, :D_MODEL] * y_d + gates[:, D_MODEL:] * y_m
        out = jnp.dot(mixed.astype(jnp.bfloat16), wo_ref[...], preferred_element_type=jnp.float32)
        o_ref[rows, :] = _layer_norm(DEEPNORM_ALPHA * h_ref[rows, :] + out, g_ref[...], b_ref[...])


def _merge(attn_d, attn_m, gl, b_gate, h, wbd, wbm, wout, ln_g, ln_b):
    n_rows = h.shape[0]
    tm = ROW_TILE
    row = lambda cols: pl.BlockSpec((tm, cols), lambda i: (i, 0))
    return pl.pallas_call(
        _merge_kernel,
        grid=(n_rows // tm,),
        in_specs=[row(D_MODEL), row(MOBA_HEADS * HEAD_DIM), row(2 * D_MODEL),
                  _resident((1, 2 * D_MODEL)), row(D_MODEL),
                  _resident(wbd.shape), _resident(wbm.shape), _resident(wout.shape),
                  _resident((1, D_MODEL)), _resident((1, D_MODEL))],
        out_specs=row(D_MODEL),
        out_shape=jax.ShapeDtypeStruct((n_rows, D_MODEL), jnp.float32),
        compiler_params=_compiler_params(1),
        name="merge",
    )(attn_d, attn_m, gl, b_gate, h, wbd, wbm, wout, ln_g, ln_b)


def _xattn_kv_kernel(mem_ref, wk_ref, wv_ref, k_ref, v_ref):
    mb = mem_ref[...].astype(jnp.bfloat16)
    k_ref[...] = jnp.dot(mb, wk_ref[...], preferred_element_type=jnp.float32).astype(k_ref.dtype)
    v_ref[...] = jnp.dot(mb, wv_ref[...], preferred_element_type=jnp.float32).astype(v_ref.dtype)


def _xattn_kv(mem2, wk, wv, batch, n_mem):
    blk = pl.BlockSpec((n_mem, D_MODEL), lambda b: (b, 0))
    return pl.pallas_call(
        _xattn_kv_kernel,
        grid=(batch,),
        in_specs=[blk, _resident(wk.shape), _resident(wv.shape)],
        out_specs=[blk, blk],
        out_shape=[jax.ShapeDtypeStruct(mem2.shape, jnp.bfloat16)] * 2,
        compiler_params=_compiler_params(1),
        name="xattn_kv",
    )(mem2, wk, wv)


def _xattn_kernel(h_ref, wq_ref, k_ref, v_ref, wo_ref, g_ref, b_ref, o_ref):
    heads = [slice(hd * XATTN_HEAD_DIM, (hd + 1) * XATTN_HEAD_DIM) for hd in range(XATTN_HEADS)]
    h = h_ref[...]
    q = jnp.dot(h.astype(jnp.bfloat16), wq_ref[...], preferred_element_type=jnp.float32)
    q = (q * XATTN_HEAD_DIM ** -0.5).astype(jnp.bfloat16)
    logits = [_qk(q[:, cols], k_ref[:, cols]) for cols in heads]
    probs = []
    for s in logits:
        p = jnp.exp(s - jnp.max(s, axis=1, keepdims=True))
        probs.append((p / jnp.sum(p, axis=1, keepdims=True)).astype(jnp.bfloat16))
    outs = [jnp.dot(p, v_ref[:, cols], preferred_element_type=jnp.float32)
            for p, cols in zip(probs, heads)]
    o = jnp.concatenate(outs, axis=1).astype(jnp.bfloat16)
    xa = jnp.dot(o, wo_ref[...], preferred_element_type=jnp.float32)
    o_ref[...] = _layer_norm(DEEPNORM_ALPHA * h + xa, g_ref[...], b_ref[...])


def _xattn(h1, wq, kx, vx, wo, ln_g, ln_b, batch, seq, n_mem):
    tm = ROW_TILE
    nq = seq // tm
    row = pl.BlockSpec((tm, D_MODEL), lambda b, i: (b * nq + i, 0))
    kv = pl.BlockSpec((n_mem, D_MODEL), lambda b, i: (b, 0))
    return pl.pallas_call(
        _xattn_kernel,
        grid=(batch, nq),
        in_specs=[row, _resident(wq.shape), kv, kv, _resident(wo.shape),
                  _resident((1, D_MODEL)), _resident((1, D_MODEL))],
        out_specs=row,
        out_shape=jax.ShapeDtypeStruct(h1.shape, jnp.float32),
        compiler_params=_compiler_params(2),
        name="xattn",
    )(h1, wq, kx, vx, wo, ln_g, ln_b)


def _mlp_kernel(h_ref, w1_ref, w2_ref, g_ref, b_ref, o_ref):
    for rows in _row_chains(o_ref.shape[0]):
        h = h_ref[rows, :]
        hb = h.astype(jnp.bfloat16)
        ff = None
        for c0 in range(0, D_FF, FF_TILE):
            u = jnp.dot(hb, w1_ref[:, c0:c0 + FF_TILE], preferred_element_type=jnp.float32)
            u = jnp.square(jnp.maximum(u, 0.0)).astype(jnp.bfloat16)
            part = jnp.dot(u, w2_ref[c0:c0 + FF_TILE, :], preferred_element_type=jnp.float32)
            ff = part if ff is None else ff + part
        o_ref[rows, :] = _layer_norm(DEEPNORM_ALPHA * h + ff, g_ref[...], b_ref[...])


def _mlp(h2, w1, w2, ln_g, ln_b):
    n_rows = h2.shape[0]
    tm = MLP_ROW_TILE
    row = pl.BlockSpec((tm, D_MODEL), lambda i: (i, 0))
    return pl.pallas_call(
        _mlp_kernel,
        grid=(n_rows // tm,),
        in_specs=[row, _resident(w1.shape), _resident(w2.shape),
                  _resident((1, D_MODEL)), _resident((1, D_MODEL))],
        out_specs=row,
        out_shape=jax.ShapeDtypeStruct(h2.shape, jnp.float32),
        compiler_params=_compiler_params(1),
        name="mlp",
    )(h2, w1, w2, ln_g, ln_b)


def kernel(x, mem, ln_in_g, ln_in_b, rel_table, w_in, b_gate, lam_q1, lam_k1, lam_q2, lam_k2,
           diff_sub_g, w_br_diff, w_br_moba, w_out, ln1_g, ln1_b, wq_x, wk_x, wv_x, wo_x,
           ln2_g, ln2_b, w_ff1, w_ff2, ln3_g, ln3_b):
    batch, seq, d = x.shape
    n_mem = mem.shape[1]
    assert d == D_MODEL and w_in.shape == (DEPTH, D_MODEL, W_IN_COLS)
    assert seq % ROW_TILE == 0 and seq % DIFF_TILE == 0 and seq % SEL_TILE == 0
    assert (batch * seq) % MLP_ROW_TILE == 0
    assert seq % MOBA_TILE == 0 and MOBA_TILE % MOBA_BLOCK == 0
    assert seq // MOBA_BLOCK <= LANES // 2 and MOBA_TILE >= REL_FAR_DIST <= DIFF_TILE
    bf = lambda w: w.astype(jnp.bfloat16)
    vec = lambda v: v.reshape(1, -1).astype(jnp.float32)
    l = 0

    x2 = x.reshape(batch * seq, d)
    h, qkv, mqf, gl, kmean = _ln_inproj(x2, vec(ln_in_g), vec(ln_in_b), bf(w_in[l]))

    n_blocks = seq // MOBA_BLOCK
    km = kmean.reshape(batch, n_blocks, MOBA_HEADS // 2, 2, HEAD_DIM)
    km = jnp.pad(km, ((0, 0), (0, LANES // 2 - n_blocks), (0, 0), (0, 0), (0, 0)))
    eye2 = jnp.eye(2, dtype=jnp.float32)
    km2 = jnp.einsum('bnpad,ac->bpcnad', km, eye2).reshape(batch, MOBA_HEADS // 2, LANES, LANES)
    neg = _moba_select(mqf, km2, batch, seq)

    rel_t = rel_table.T.astype(jnp.float32)
    lam_p = jnp.concatenate([lam_q1[l:l + 1], lam_k1[l:l + 1], lam_q2[l:l + 1], lam_k2[l:l + 1]],
                            axis=0).astype(jnp.float32)
    attn_d = _diff_attn(rel_t, qkv, lam_p, vec(diff_sub_g[l]), batch, seq)
    attn_m = _moba_attn(rel_t, qkv, neg, batch, seq)

    h1 = _merge(attn_d, attn_m, gl, vec(b_gate[l]), h, bf(w_br_diff[l]), bf(w_br_moba[l]),
                bf(w_out[l]), vec(ln1_g[l]), vec(ln1_b[l]))

    kx, vx = _xattn_kv(mem.reshape(batch * n_mem, d), bf(wk_x[l]), bf(wv_x[l]), batch, n_mem)
    h2 = _xattn(h1, bf(wq_x[l]), kx, vx, bf(wo_x[l]), vec(ln2_g[l]), vec(ln2_b[l]),
                batch, seq, n_mem)

    h3 = _mlp(h2, bf(w_ff1[l]), bf(w_ff2[l]), vec(ln3_g[l]), vec(ln3_b[l]))
    return h3.reshape(batch, seq, d)
```
